```python
import math
import jax, jax.numpy as jnp
from jax import lax
import numpy as np

D_MODEL = 2048
BATCH = 1
SEQ = 16384
DEPTH = 1

GRID_W = 64
CTX_LEN = 256
D_MIX = D_MODEL
ATTN_WIDTH = D_MIX // 2
REC_WIDTH = D_MIX - ATTN_WIDTH
N_HEADS = 8
DV = ATTN_WIDTH // N_HEADS
HEAD_DIM = DV // 2
N_FREQ = HEAD_DIM // 4
N_REC_BLOCKS = 8
REC_BLOCK = REC_WIDTH // N_REC_BLOCKS
REC_CONV = 4
REC_PAD = (2, 1)
RG_C = 8.0
D_FF = ((8 * D_MODEL // 3 + 127) // 128) * 128
FFN_CONV = 3
FFN_PAD = (1, 1)
IN_COLS = 3 * ATTN_WIDTH + 2 * REC_WIDTH
Q_BLOCK = 128
ROPE_THETA = 10000.0
EPS = 1e-6
SUBLN_EPS = 1e-5

kernel_name = 'hymba_diffattn_rglru_convffn_dit'


def rms_norm(t, g, eps=EPS):
    tf = t.astype(jnp.float32)
    y = tf * lax.rsqrt(jnp.mean(tf * tf, axis=-1, keepdims=True) + eps)
    return (y * g).astype(t.dtype)


def modulate(h, shift, scale):
    return h * (1 + scale) + shift


def dwconv(t, w, b, pad):
    n = t.shape[1]
    tp = jnp.pad(t, ((0, 0), pad, (0, 0)))
    y = b
    for j in range(w.shape[0]):
        y = y + tp[:, j:j + n] * w[j]
    return y


def split_in(p):
    b, n = p.shape[:2]
    q, k, v, rx, rz = jnp.split(p, [ATTN_WIDTH, 2 * ATTN_WIDTH, 3 * ATTN_WIDTH,
                                    3 * ATTN_WIDTH + REC_WIDTH], axis=-1)
    q = q.reshape(b, n, N_HEADS, 2, HEAD_DIM)
    k = k.reshape(b, n, N_HEADS, 2, HEAD_DIM)
    v = v.reshape(b, n, N_HEADS, DV)
    return q, k, v, rx, rz


def axial_rope_tables(row, col):
    inv = ROPE_THETA ** (-jnp.arange(N_FREQ, dtype=jnp.float32) / N_FREQ)
    ang = jnp.stack([row.astype(jnp.float32)[:, None] * inv,
                     col.astype(jnp.float32)[:, None] * inv], axis=1)
    return jnp.cos(ang), jnp.sin(ang)


def apply_rope(t, cos, sin):
    shp = t.shape
    tr = t.astype(jnp.float32).reshape(shp[:-1] + (2, 2, N_FREQ))
    t1, t2 = tr[..., 0, :], tr[..., 1, :]
    cs, sn = cos[None, :, None, None], sin[None, :, None, None]
    out = jnp.stack([t1 * cs - t2 * sn, t2 * cs + t1 * sn], axis=-2)
    return out.reshape(shp).astype(t.dtype)


def diff_lambda_value(dl, lam_init):
    d = dl.astype(jnp.float32)
    return jnp.exp(jnp.sum(d[0] * d[1])) - jnp.exp(jnp.sum(d[2] * d[3])) + lam_init


def diff_attend(q, k, v, lam):
    s = jnp.einsum('bqhcd,bkhcd->bhcqk', q, k).astype(jnp.float32) * (HEAD_DIM ** -0.5)
    p = jax.nn.softmax(s, axis=-1)
    w = p[:, :, 0] - lam * p[:, :, 1]
    return jnp.einsum('bhqk,bkhd->bqhd', w.astype(v.dtype), v)


def blocked_diff_attention(q, k, v, lam):
    b, s = q.shape[:2]
    nb = s // Q_BLOCK
    qb = jnp.moveaxis(q.reshape((b, nb, Q_BLOCK) + q.shape[2:]), 1, 0)
    out = lax.map(lambda qblk: diff_attend(qblk, k, v, lam), qb)
    return jnp.moveaxis(out, 0, 1).reshape((b, s) + out.shape[3:])


def finish_attn(o, g, lam_init):
    b, n = o.shape[:2]
    o = rms_norm(o, g, SUBLN_EPS) * (1 - lam_init)
    return o.reshape(b, n, ATTN_WIDTH)


def block_diag(t, w):
    b, n, _ = t.shape
    tb = t.reshape(b, n, N_REC_BLOCKS, REC_BLOCK)
    return jnp.einsum('bnkc,kcd->bnkd', tb, w).reshape(b, n, REC_WIDTH)


def _lin_combine(left, right):
    a_l, b_l = left
    a_r, b_r = right
    return a_l * a_r, a_r * b_l + b_r


def rglru_scan(t, wa, ba, wi, bi, lam, h0):
    r = jax.nn.sigmoid(block_diag(t, wa).astype(jnp.float32) + ba)
    i = jax.nn.sigmoid(block_diag(t, wi).astype(jnp.float32) + bi)
    log_a = -RG_C * r * jax.nn.softplus(-lam.astype(jnp.float32))
    a = jnp.exp(log_a)
    bvals = jnp.sqrt(-jnp.expm1(2 * log_a)) * (i * t.astype(jnp.float32))
    if h0 is not None:
        bvals = bvals.at[:, 0].add(a[:, 0] * h0)
    _, h = lax.associative_scan(_lin_combine, (a, bvals), axis=1)
    return h


def flip(t, d):
    return t[:, ::-1] if d == 1 else t


def conv_ffn(h, w_up, w_gate, cw, cb, w_down):
    u = h @ w_up
    g = dwconv(h @ w_gate, cw, cb, FFN_PAD)
    return (jax.nn.gelu(g, approximate=True) * u) @ w_down


def setup_inputs(seed: int = 0) -> dict:
    key = jax.random.key(seed)
    ks = jax.random.split(key, 26)
    f32 = jnp.float32
    L = DEPTH

    def nrm(k, shape, scale):
        return jax.random.normal(k, shape, f32) * scale

    a0 = jax.random.uniform(ks[14], (L, 2, REC_WIDTH), f32, 0.9, 0.999)
    s0 = a0 ** (1.0 / RG_C)
    return {
        'x': nrm(ks[0], (BATCH, SEQ, D_MODEL), 1.0),
        'c': nrm(ks[1], (BATCH, D_MODEL), 1.0),
        'ctx': nrm(ks[2], (BATCH, CTX_LEN, D_MODEL), 1.0),
        'c_ctx': nrm(ks[3], (D_MODEL,), 1.0),
        'w_ada': nrm(ks[4], (L, D_MODEL, 6 * D_MODEL), 0.5 * D_MODEL ** -0.5),
        'b_ada': nrm(ks[5], (L, 6 * D_MODEL), 0.02),
        'norm1_g': 1 + nrm(ks[6], (L, D_MODEL), 0.05),
        'w_in': nrm(ks[7], (L, D_MODEL, IN_COLS), D_MODEL ** -0.5),
        'rec_conv_w': nrm(ks[8], (L, REC_CONV, REC_WIDTH), REC_CONV ** -0.5),
        'rec_conv_b': nrm(ks[9], (L, REC_WIDTH), 0.02),
        'rg_wa': nrm(ks[10], (L, 2, N_REC_BLOCKS, REC_BLOCK, REC_BLOCK), REC_BLOCK ** -0.5),
        'rg_ba': nrm(ks[11], (L, 2, REC_WIDTH), 0.02),
        'rg_wi': nrm(ks[12], (L, 2, N_REC_BLOCKS, REC_BLOCK, REC_BLOCK), REC_BLOCK ** -0.5),
        'rg_bi': nrm(ks[13], (L, 2, REC_WIDTH), 0.02),
        'rg_lambda': jnp.log(s0) - jnp.log1p(-s0),
        'diff_lambda': nrm(ks[15], (L, 4, HEAD_DIM), 0.1),
        'subln_g': 1 + nrm(ks[16], (L, DV), 0.05),
        'w_out': nrm(ks[17], (L, D_MIX, D_MODEL), D_MIX ** -0.5),
        'norm2_g': 1 + nrm(ks[18], (L, D_MODEL), 0.05),
        'w_up': nrm(ks[19], (L, D_MODEL, D_FF), D_MODEL ** -0.5),
        'w_gate': nrm(ks[20], (L, D_MODEL, D_FF), D_MODEL ** -0.5),
        'ffn_conv_w': nrm(ks[21], (L, FFN_CONV, D_FF), FFN_CONV ** -0.5),
        'ffn_conv_b': nrm(ks[22], (L, D_FF), 0.02),
        'w_down': nrm(ks[23], (L, D_FF, D_MODEL), D_FF ** -0.5),
        'final_g': 1 + nrm(ks[24], (D_MODEL,), 0.05),
    }


def reference(x, c, ctx, c_ctx, w_ada, b_ada, norm1_g, w_in, rec_conv_w, rec_conv_b,
              rg_wa, rg_ba, rg_wi, rg_bi, rg_lambda, diff_lambda, subln_g, w_out,
              norm2_g, w_up, w_gate, ffn_conv_w, ffn_conv_b, w_down, final_g):
    n_lat = x.shape[1]
    rows = n_lat // GRID_W
    row, col = jnp.meshgrid(jnp.arange(rows), jnp.arange(GRID_W), indexing='ij')
    cos, sin = axial_rope_tables(row.reshape(-1), col.reshape(-1))
    cx = ctx
    for l in range(DEPTH):
        last = l == DEPTH - 1
        lam_init = 0.8 - 0.6 * math.exp(-0.3 * l)
        mod = (jax.nn.silu(c) @ w_ada[l] + b_ada[l])[:, None, :]
        mod_c = (jax.nn.silu(c_ctx) @ w_ada[l] + b_ada[l])[None, None, :]
        sh1, sc1, g1, sh2, sc2, g2 = jnp.split(mod, 6, axis=-1)
        csh1, csc1, cg1, csh2, csc2, cg2 = jnp.split(mod_c, 6, axis=-1)

        h = modulate(rms_norm(x, norm1_g[l]), sh1, sc1)
        hc = modulate(rms_norm(cx, norm1_g[l]), csh1, csc1)
        q, k, v, rx, rz = split_in(h @ w_in[l])
        qc, kc, vc, rxc, rzc = split_in(hc @ w_in[l])

        lam = diff_lambda_value(diff_lambda[l], lam_init)
        q = apply_rope(q, cos, sin)
        k = apply_rope(k, cos, sin)
        k_all = jnp.concatenate([kc, k], axis=1)
        v_all = jnp.concatenate([vc, v], axis=1)
        attn = finish_attn(blocked_diff_attention(q, k_all, v_all, lam), subln_g[l], lam_init)

        xr = dwconv(rx, rec_conv_w[l], rec_conv_b[l], REC_PAD)
        xrc = dwconv(rxc, rec_conv_w[l], rec_conv_b[l], REC_PAD)
        rec = 0.0
        rec_c = 0.0
        for d in range(2):
            prm = (rg_wa[l, d], rg_ba[l, d], rg_wi[l, d], rg_bi[l, d], rg_lambda[l, d])
            h_ctx = rglru_scan(flip(xrc, d), *prm, None)
            h_lat = rglru_scan(flip(xr, d), *prm, h_ctx[:, -1])
            rec = rec + flip(h_lat, d)
            if not last:
                rec_c = rec_c + flip(h_ctx, d)
        rec = (rec * jax.nn.gelu(rz, approximate=True)).astype(x.dtype)
        x = x + g1 * (jnp.concatenate([attn, rec], axis=-1) @ w_out[l])
        if not last:
            attn_c = finish_attn(diff_attend(qc, kc, vc, lam), subln_g[l], lam_init)
            rec_c = (rec_c * jax.nn.gelu(rzc, approximate=True)).astype(cx.dtype)
            cx = cx + cg1 * (jnp.concatenate([attn_c, rec_c], axis=-1) @ w_out[l])

        h2 = modulate(rms_norm(x, norm2_g[l]), sh2, sc2)
        x = x + g2 * conv_ffn(h2, w_up[l], w_gate[l], ffn_conv_w[l], ffn_conv_b[l], w_down[l])
        if not last:
            hc2 = modulate(rms_norm(cx, norm2_g[l]), csh2, csc2)
            cx = cx + cg2 * conv_ffn(hc2, w_up[l], w_gate[l], ffn_conv_w[l], ffn_conv_b[l], w_down[l])
    return rms_norm(x, final_g)
```

```python
import functools
import math

import jax
import jax.numpy as jnp
from jax import lax
from jax.experimental import pallas as pl
from jax.experimental.pallas import tpu as pltpu

F32 = jnp.float32
BF16 = jnp.bfloat16

GRID_W = 64
N_HEADS = 8
DV = 128
HEAD_DIM = DV // 2
N_FREQ = HEAD_DIM // 4
N_REC_BLOCKS = 8
REC_BLOCK = 128
RG_C = 8.0
ROPE_THETA = 10000.0
EPS = 1e-6
SUBLN_EPS = 1e-5
LAM_INIT = 0.8 - 0.6 * math.exp(-0.3 * 0)
LOG2E = 1.4426950408889634
NEG_BIG = -1e30

LANES = 128
MXU_N = 256
VMEM_LIMIT = 56 * 1024 * 1024


def _params(sem, vmem=VMEM_LIMIT):
    return pltpu.CompilerParams(dimension_semantics=sem, vmem_limit_bytes=vmem)


def _tile(n, pref):
    if n <= pref:
        return n
    t = pref
    while t >= 16:
        if n % t == 0:
            return t
        t -= 16
    return n


def _gelu_tanh(x):
    return 0.5 * x * (1.0 + jnp.tanh(math.sqrt(2.0 / math.pi) * (x + 0.044715 * (x * x * x))))


def _rms(x, eps):
    return x * lax.rsqrt(jnp.mean(x * x, axis=-1, keepdims=True) + eps)


def _ada_kernel(c_ref, w_ref, b_ref, o_ref):
    cc = c_ref[...]
    s = cc * jax.nn.sigmoid(cc)
    o_ref[...] = jnp.dot(s, w_ref[...], preferred_element_type=F32,
                         precision=lax.Precision.HIGHEST) + b_ref[...]


def _ada(cc, w_ada, b_ada):
    d, n = w_ada.shape
    tn = _tile(n, 1024)
    return pl.pallas_call(
        _ada_kernel,
        grid=(n // tn,),
        in_specs=[pl.BlockSpec((8, d), lambda j: (0, 0)),
                  pl.BlockSpec((d, tn), lambda j: (0, j)),
                  pl.BlockSpec((1, tn), lambda j: (0, j))],
        out_specs=pl.BlockSpec((8, tn), lambda j: (0, j)),
        out_shape=jax.ShapeDtypeStruct((8, n), F32),
        compiler_params=_params(("arbitrary",)),
        name="ada",
    )(cc, w_ada, b_ada.reshape(1, n))


def _inproj_kernel(x_ref, g_ref, sh_ref, sc_ref, w_ref, cos_ref, sin_ref,
                   q_ref, k_ref, v_ref, rx_ref, rz_ref, h_scr, *, q_scale):
    j = pl.program_id(1)
    tm = x_ref.shape[0]

    @pl.when(j == 0)
    def _():
        y = _rms(x_ref[...], EPS) * g_ref[...]
        h_scr[...] = (y * (1.0 + sc_ref[...]) + sh_ref[...]).astype(BF16)

    acc = jnp.dot(h_scr[...], w_ref[...], preferred_element_type=F32)

    def rope_into(o_ref, scale):
        cos = cos_ref[...]
        sin = sin_ref[...]
        lane = lax.broadcasted_iota(jnp.int32, (tm, LANES), 1)
        first = (lane % 32) < 16
        for g in range(acc.shape[1] // LANES):
            t = acc[:, g * LANES:(g + 1) * LANES]
            partner = jnp.where(first, pltpu.roll(t, LANES - 16, 1), pltpu.roll(t, 16, 1))
            r = t * cos + partner * sin
            if scale != 1.0:
                r = r * scale
            o_ref[:, g * LANES:(g + 1) * LANES] = r.astype(o_ref.dtype)

    @pl.when(j == 0)
    def _():
        rope_into(q_ref, q_scale)

    @pl.when(j == 1)
    def _():
        rope_into(k_ref, 1.0)

    @pl.when(j == 2)
    def _():
        v_ref[...] = acc.astype(v_ref.dtype)

    @pl.when(j == 3)
    def _():
        rx_ref[...] = acc

    @pl.when(j == 4)
    def _():
        rz_ref[...] = acc


def _inproj(x2, g, sh, sc, w_in, cos, sin, tm):
    s, d = x2.shape
    wcol = w_in.shape[1] // 5
    row = lambda i, j: (i, 0)
    vec = pl.BlockSpec((1, d), lambda i, j: (0, 0))
    out = lambda dt: jax.ShapeDtypeStruct((s, wcol), dt)
    return pl.pallas_call(
        functools.partial(_inproj_kernel, q_scale=HEAD_DIM ** -0.5 * LOG2E),
        grid=(s // tm, 5),
        in_specs=[pl.BlockSpec((tm, d), row), vec, vec, vec,
                  pl.BlockSpec((d, wcol), lambda i, j: (0, j)),
                  pl.BlockSpec((tm, LANES), row), pl.BlockSpec((tm, LANES), row)],
        out_specs=[pl.BlockSpec((tm, wcol), row)] * 5,
        out_shape=[out(BF16), out(BF16), out(BF16), out(F32), out(F32)],
        scratch_shapes=[pltpu.VMEM((tm, d), BF16)],
        compiler_params=_params(("arbitrary", "arbitrary")),
        name="inproj",
    )(x2, g, sh, sc, w_in, cos, sin)


def _attn_kernel(q_ref, k_ref, v_ref, dl_ref, sg_ref, o_ref, vaug, qq, acc, m_scr,
                 *, tkc, n_chunks):
    tq = q_ref.shape[0]

    @pl.when(pl.program_id(1) == 0)
    def _():
        vaug[:, :DV] = v_ref[...]
        vaug[:, DV:] = jnp.ones((vaug.shape[0], MXU_N - DV), BF16)

    q = q_ref[...]
    lane = lax.broadcasted_iota(jnp.int32, q.shape, 1)
    zero = jnp.zeros_like(q)
    qq[:tq, :] = jnp.where(lane < HEAD_DIM, q, zero)
    qq[tq:, :] = jnp.where(lane >= HEAD_DIM, q, zero)
    m_scr[...] = jnp.full(m_scr.shape, NEG_BIG, F32)
    acc[...] = jnp.zeros(acc.shape, F32)

    def body(c, carry):
        off = pl.multiple_of(c * tkc, tkc)
        s = lax.dot_general(qq[...], k_ref[pl.ds(off, tkc), :],
                            (((1,), (1,)), ((), ())), preferred_element_type=F32)
        m_prev = m_scr[...]
        m_new = jnp.maximum(m_prev, jnp.max(s, axis=1, keepdims=True))
        p = jnp.exp2(s - pltpu.repeat(m_new, tkc // LANES, 1))
        alpha = jnp.exp2(m_prev - m_new)
        pv = jnp.dot(p.astype(BF16), vaug[pl.ds(off, tkc), :], preferred_element_type=F32)
        acc[...] = acc[...] * pltpu.repeat(alpha, MXU_N // LANES, 1) + pv
        m_scr[...] = m_new
        return carry

    lax.fori_loop(0, n_chunks, body, 0)

    d = dl_ref[...]
    lam = (jnp.exp(jnp.sum(d[0:1] * d[1:2], axis=1, keepdims=True))
           - jnp.exp(jnp.sum(d[2:3] * d[3:4], axis=1, keepdims=True)) + LAM_INIT)
    a = acc[...]
    o = a[:tq, :DV] / a[:tq, DV:] - lam * (a[tq:, :DV] / a[tq:, DV:])
    o_ref[...] = (_rms(o, SUBLN_EPS) * sg_ref[...] * (1.0 - LAM_INIT)).astype(o_ref.dtype)


def _attn(q, k_all, v_all, diff_lambda, subln_g, tq, tkc):
    s = q.shape[0]
    n_kv = k_all.shape[0]
    return pl.pallas_call(
        functools.partial(_attn_kernel, tkc=tkc, n_chunks=n_kv // tkc),
        grid=(N_HEADS, s // tq),
        in_specs=[pl.BlockSpec((tq, DV), lambda h, i: (i, h)),
                  pl.BlockSpec((n_kv, DV), lambda h, i: (0, h)),
                  pl.BlockSpec((n_kv, DV), lambda h, i: (0, h)),
                  pl.BlockSpec((4, HEAD_DIM), lambda h, i: (0, 0)),
                  pl.BlockSpec((1, DV), lambda h, i: (0, 0))],
        out_specs=pl.BlockSpec((tq, DV), lambda h, i: (i, h)),
        out_shape=jax.ShapeDtypeStruct((s, N_HEADS * DV), BF16),
        scratch_shapes=[pltpu.VMEM((n_kv, MXU_N), BF16),
                        pltpu.VMEM((2 * tq, DV), BF16),
                        pltpu.VMEM((2 * tq, MXU_N), F32),
                        pltpu.VMEM((2 * tq, LANES), F32)],
        compiler_params=_params(("arbitrary", "arbitrary")),
        name="attn",
    )(q, k_all, v_all, diff_lambda, subln_g)


def _log1p(u):
    w = 1.0 + u
    return jnp.where(w == 1.0, u, jnp.log(w) * (u / (w - 1.0)))


def _softplus(z):
    return jnp.maximum(z, 0.0) + _log1p(jnp.exp(-jnp.abs(z)))


def _neg_expm1(u):
    e = jnp.exp(u)
    d = e - 1.0
    safe = jnp.where(d == 0.0, 1.0, jnp.log(e))
    return -jnp.where(d == 0.0, u, d * (u / safe))


def _rglru_kernel(cur_f, prev_f, next_f, cur_b, prev_b, next_b, cw_ref, cb_ref, w_ref,
                  ba_ref, bi_ref, lam_ref, h0_ref, hf_ref, hb_ref, hl_ref,
                  e_scr, a_f, b_f, a_b, b_b, h_scr, *, halo):
    i = pl.program_id(0)
    n = pl.num_programs(0)
    t_rows = cur_f.shape[0]

    @pl.when(i == 0)
    def _():
        h_scr[...] = h0_ref[...]

    def conv(cur, prev, nxt, first, last):
        e_scr[0:halo, :] = jnp.where(first, 0.0, prev[...])
        e_scr[halo:halo + t_rows, :] = cur[...]
        e_scr[halo + t_rows:, :] = jnp.where(last, 0.0, nxt[...])
        y = cb_ref[...] + cw_ref[0:1, :] * e_scr[halo - 2:halo - 2 + t_rows, :]
        y = y + cw_ref[1:2, :] * e_scr[halo - 1:halo - 1 + t_rows, :]
        y = y + cw_ref[2:3, :] * e_scr[halo:halo + t_rows, :]
        y = y + cw_ref[3:4, :] * e_scr[halo + 1:halo + 1 + t_rows, :]
        return y

    def gates(xr, d, a_scr, b_scr):
        xb = xr.astype(BF16)
        sp = _softplus(-lam_ref[d:d + 1, :])
        for k in range(N_REC_BLOCKS):
            sl = slice(k * REC_BLOCK, (k + 1) * REC_BLOCK)
            pre = jnp.dot(xb[:, sl], w_ref[d, k], preferred_element_type=F32)
            r = jax.nn.sigmoid(pre[:, :REC_BLOCK] + ba_ref[d:d + 1, sl])
            ig = jax.nn.sigmoid(pre[:, REC_BLOCK:] + bi_ref[d:d + 1, sl])
            log_a = (-RG_C) * r * sp[:, sl]
            a_scr[:, sl] = jnp.exp(log_a)
            b_scr[:, sl] = jnp.sqrt(_neg_expm1(2.0 * log_a)) * (ig * xr[:, sl])

    gates(conv(cur_f, prev_f, next_f, i == 0, i == n - 1), 0, a_f, b_f)
    gates(conv(cur_b, prev_b, next_b, i == n - 1, i == 0), 1, a_b, b_b)

    def step(t, carry):
        hf, hb = carry
        hf = a_f[pl.ds(t, 1), :] * hf + b_f[pl.ds(t, 1), :]
        hf_ref[pl.ds(t, 1), :] = hf
        tb = t_rows - 1 - t
        hb = a_b[pl.ds(tb, 1), :] * hb + b_b[pl.ds(tb, 1), :]
        hb_ref[pl.ds(tb, 1), :] = hb
        return hf, hb

    hf, hb = lax.fori_loop(0, t_rows, step, (h_scr[0:1, :], h_scr[1:2, :]), unroll=8)
    h_scr[0:1, :] = hf
    h_scr[1:2, :] = hb
    hl_ref[...] = h_scr[...]


def _rglru(rx, cw, cb, wcat, ba, bi, lam, h0, t_rows):
    s, c = rx.shape
    n = s // t_rows
    halo = 8
    bpc = t_rows // halo
    nhb = s // halo
    cur = lambda f: pl.BlockSpec((t_rows, c), lambda i: (f(i, n), 0))
    prv = lambda f: pl.BlockSpec((halo, c), lambda i: (jnp.maximum(f(i, n) * bpc - 1, 0), 0))
    nxt = lambda f: pl.BlockSpec((halo, c), lambda i: (jnp.minimum((f(i, n) + 1) * bpc, nhb - 1), 0))
    fwd = lambda i, n: i
    bwd = lambda i, n: n - 1 - i
    full = lambda shape: pl.BlockSpec(shape, lambda i: (0,) * len(shape))
    return pl.pallas_call(
        functools.partial(_rglru_kernel, halo=halo),
        grid=(n,),
        in_specs=[cur(fwd), prv(fwd), nxt(fwd), cur(bwd), prv(bwd), nxt(bwd),
                  full(cw.shape), full((1, c)), full(wcat.shape),
                  full((2, c)), full((2, c)), full((2, c)), full((8, c))],
        out_specs=[pl.BlockSpec((t_rows, c), lambda i: (i, 0)),
                   pl.BlockSpec((t_rows, c), lambda i: (n - 1 - i, 0)),
                   full((8, c))],
        out_shape=[jax.ShapeDtypeStruct((s, c), F32), jax.ShapeDtypeStruct((s, c), F32),
                   jax.ShapeDtypeStruct((8, c), F32)],
        scratch_shapes=[pltpu.VMEM((t_rows + 2 * halo, c), F32)]
                       + [pltpu.VMEM((t_rows, c), F32)] * 4
                       + [pltpu.VMEM((8, c), F32)],
        compiler_params=_params(("arbitrary",)),
        name="rglru",
    )(rx, rx, rx, rx, rx, rx, cw, cb.reshape(1, c), wcat, ba, bi, lam, h0)


def _outproj_kernel(attn_ref, hf_ref, hb_ref, rz_ref, x_ref, w_ref, g1_ref, n2_ref,
                    sh_ref, sc_ref, x1_ref, h2_ref):
    half = attn_ref.shape[1]
    rec = ((hf_ref[...] + hb_ref[...]) * _gelu_tanh(rz_ref[...])).astype(BF16)
    y = jnp.dot(attn_ref[...], w_ref[:half, :], preferred_element_type=F32)
    y = y + jnp.dot(rec, w_ref[half:, :], preferred_element_type=F32)
    x1 = x_ref[...] + g1_ref[...] * y
    x1_ref[...] = x1
    h2 = _rms(x1, EPS) * n2_ref[...]
    h2_ref[...] = (h2 * (1.0 + sc_ref[...]) + sh_ref[...]).astype(BF16)


def _outproj(attn, hf, hb, rz, x2, w_out, g1, n2, sh2, sc2, tm):
    s, d = x2.shape
    half = attn.shape[1]
    row = lambda i: (i, 0)
    vec = pl.BlockSpec((1, d), lambda i: (0, 0))
    return pl.pallas_call(
        _outproj_kernel,
        grid=(s // tm,),
        in_specs=[pl.BlockSpec((tm, half), row)] * 4
                 + [pl.BlockSpec((tm, d), row), pl.BlockSpec(w_out.shape, lambda i: (0, 0)),
                    vec, vec, vec, vec],
        out_specs=[pl.BlockSpec((tm, d), row), pl.BlockSpec((tm, d), row)],
        out_shape=[jax.ShapeDtypeStruct((s, d), F32), jax.ShapeDtypeStruct((s, d), BF16)],
        compiler_params=_params(("arbitrary",)),
        name="outproj",
    )(attn, hf, hb, rz, x2, w_out, g1, n2, sh2, sc2)


def _ffn_kernel(h_ref, hp_ref, hn_ref, x1_ref, wu_ref, wg_ref, wd_ref, cw_ref, cb_ref,
                g2_ref, fg_ref, o_ref, he, acc, *, halo):
    i = pl.program_id(0)
    f = pl.program_id(1)
    tm = h_ref.shape[0]

    @pl.when(f == 0)
    def _():
        zero = jnp.zeros(hp_ref.shape, BF16)
        he[0:halo, :] = jnp.where(i == 0, zero, hp_ref[...])
        he[halo:halo + tm, :] = h_ref[...]
        he[halo + tm:, :] = jnp.where(i == pl.num_programs(0) - 1, zero, hn_ref[...])
        acc[...] = jnp.zeros(acc.shape, F32)

    u = jnp.dot(h_ref[...], wu_ref[...], preferred_element_type=F32)
    g = jnp.dot(he[...], wg_ref[...], preferred_element_type=F32)
    rows = g.shape[0]
    gm = pltpu.roll(g, 1, 0)
    gp = pltpu.roll(g, rows - 1, 0)
    conv = cb_ref[...] + cw_ref[0:1, :] * gm + cw_ref[1:2, :] * g + cw_ref[2:3, :] * gp
    act = (_gelu_tanh(conv[halo:halo + tm, :]) * u).astype(BF16)
    acc[...] += jnp.dot(act, wd_ref[...], preferred_element_type=F32)

    @pl.when(f == pl.num_programs(1) - 1)
    def _():
        x2 = x1_ref[...] + g2_ref[...] * acc[...]
        o_ref[...] = _rms(x2, EPS) * fg_ref[...]


def _ffn(h2, x1, w_up, w_gate, w_down, cw, cb, g2, fg, tm, tf):
    s, d = h2.shape
    dff = w_up.shape[1]
    halo = 16
    bpc = tm // halo
    nhb = s // halo
    row = lambda i, f: (i, 0)
    vec = pl.BlockSpec((1, d), lambda i, f: (0, 0))
    return pl.pallas_call(
        functools.partial(_ffn_kernel, halo=halo),
        grid=(s // tm, dff // tf),
        in_specs=[pl.BlockSpec((tm, d), row),
                  pl.BlockSpec((halo, d), lambda i, f: (jnp.maximum(i * bpc - 1, 0), 0)),
                  pl.BlockSpec((halo, d), lambda i, f: (jnp.minimum((i + 1) * bpc, nhb - 1), 0)),
                  pl.BlockSpec((tm, d), row),
                  pl.BlockSpec((d, tf), lambda i, f: (0, f)),
                  pl.BlockSpec((d, tf), lambda i, f: (0, f)),
                  pl.BlockSpec((tf, d), lambda i, f: (f, 0)),
                  pl.BlockSpec((3, tf), lambda i, f: (0, f)),
                  pl.BlockSpec((1, tf), lambda i, f: (0, f)),
                  vec, vec],
        out_specs=pl.BlockSpec((tm, d), row),
        out_shape=jax.ShapeDtypeStruct((s, d), F32),
        scratch_shapes=[pltpu.VMEM((tm + 2 * halo, d), BF16), pltpu.VMEM((tm, d), F32)],
        compiler_params=_params(("arbitrary", "arbitrary")),
        name="ffn",
    )(h2, h2, h2, x1, w_up, w_gate, w_down, cw, cb, g2, fg)


def _rope_tables(s):
    pos = jnp.arange(s)
    inv = ROPE_THETA ** (-jnp.arange(N_FREQ, dtype=F32) / N_FREQ)
    ang_r = (pos // GRID_W).astype(F32)[:, None] * inv
    ang_c = (pos % GRID_W).astype(F32)[:, None] * inv
    cos = jnp.concatenate([jnp.cos(ang_r)] * 2 + [jnp.cos(ang_c)] * 2, axis=1)
    sin = jnp.concatenate([-jnp.sin(ang_r), jnp.sin(ang_r), -jnp.sin(ang_c), jnp.sin(ang_c)], axis=1)
    return jnp.tile(cos, (1, DV // HEAD_DIM)), jnp.tile(sin, (1, DV // HEAD_DIM))


def kernel(x, c, ctx, c_ctx, w_ada, b_ada, norm1_g, w_in, rec_conv_w, rec_conv_b, rg_wa, rg_ba,
           rg_wi, rg_bi, rg_lambda, diff_lambda, subln_g, w_out, norm2_g, w_up, w_gate,
           ffn_conv_w, ffn_conv_b, w_down, final_g):
    assert x.shape[0] == 1 and w_ada.shape[0] == 1, "single batch element, single layer"
    s, d = x.shape[1:]
    n_ctx = ctx.shape[1]
    x2 = x[0]
    cx2 = ctx[0]

    cc = jnp.zeros((8, d), F32).at[0].set(c[0]).at[1].set(c_ctx)
    mod = _ada(cc, w_ada[0], b_ada[0])
    sh1, sc1, g1, sh2, sc2, g2 = [mod[0:1, j * d:(j + 1) * d] for j in range(6)]
    csh1, csc1 = mod[1:2, 0:d], mod[1:2, d:2 * d]

    w_in_b = w_in[0].astype(BF16)
    n1 = norm1_g[0].reshape(1, d)
    cos, sin = _rope_tables(s)
    q, k, v, rx, rz = _inproj(x2, n1, sh1, sc1, w_in_b, cos, sin, _tile(s, 512))
    _, kc, vc, rxc, _ = _inproj(cx2, n1, csh1, csc1, w_in_b, jnp.ones((n_ctx, DV), F32),
                                jnp.zeros((n_ctx, DV), F32), n_ctx)

    k_all = jnp.concatenate([kc, k], axis=0)
    v_all = jnp.concatenate([vc, v], axis=0)
    n_kv = s + n_ctx
    tkc = 1280 if n_kv % 1280 == 0 else MXU_N
    attn = _attn(q, k_all, v_all, diff_lambda[0], subln_g[0].reshape(1, DV), _tile(s, 512), tkc)

    wcat = jnp.concatenate([rg_wa[0], rg_wi[0]], axis=-1).astype(BF16)
    rec_args = (rec_conv_w[0], rec_conv_b[0], wcat, rg_ba[0], rg_bi[0], rg_lambda[0])
    _, _, h_ctx = _rglru(rxc, *rec_args, jnp.zeros((8, rxc.shape[1]), F32), n_ctx)
    hf, hb, _ = _rglru(rx, *rec_args, h_ctx, _tile(s, 256))

    x1, h2 = _outproj(attn, hf, hb, rz, x2, w_out[0].astype(BF16), g1,
                      norm2_g[0].reshape(1, d), sh2, sc2, _tile(s, 512))

    dff = w_up.shape[2]
    tf = 512
    pad = (-dff) % tf
    w_up_b = jnp.pad(w_up[0].astype(BF16), ((0, 0), (0, pad)))
    w_gate_b = jnp.pad(w_gate[0].astype(BF16), ((0, 0), (0, pad)))
    w_down_b = jnp.pad(w_down[0].astype(BF16), ((0, pad), (0, 0)))
    cw = jnp.pad(ffn_conv_w[0], ((0, 0), (0, pad)))
    cb = jnp.pad(ffn_conv_b[0], ((0, pad),)).reshape(1, dff + pad)
    out = _ffn(h2, x1, w_up_b, w_gate_b, w_down_b, cw, cb, g2, final_g.reshape(1, d),
               _tile(s, 512), tf)
    return out[None]
```

```python
import functools
import math

import jax
import jax.numpy as jnp
from jax import lax
from jax.experimental import pallas as pl
from jax.experimental.pallas import tpu as pltpu

F32 = jnp.float32
BF16 = jnp.bfloat16

GRID_W = 64
N_HEADS = 8
DV = 128
HEAD_DIM = DV // 2
N_FREQ = HEAD_DIM // 4
N_REC_BLOCKS = 8
REC_BLOCK = 128
RG_C = 8.0
ROPE_THETA = 10000.0
EPS = 1e-6
SUBLN_EPS = 1e-5
LAM_INIT = 0.8 - 0.6 * math.exp(-0.3 * 0)
LOG2E = 1.4426950408889634
NEG_BIG = -1e30

LANES = 128
MXU_N = 256
VMEM_LIMIT = 56 * 1024 * 1024


def _params(sem, vmem=VMEM_LIMIT):
    return pltpu.CompilerParams(dimension_semantics=sem, vmem_limit_bytes=vmem)


def _tile(n, pref):
    if n <= pref:
        return n
    t = pref
    while t >= 16:
        if n % t == 0:
            return t
        t -= 16
    return n


def _gelu_tanh(x):
    return 0.5 * x * (1.0 + jnp.tanh(math.sqrt(2.0 / math.pi) * (x + 0.044715 * (x * x * x))))


def _rms(x, eps):
    return x * lax.rsqrt(jnp.mean(x * x, axis=-1, keepdims=True) + eps)


def _ada_kernel(c_ref, w_ref, b_ref, o_ref):
    cc = c_ref[...]
    s = cc * jax.nn.sigmoid(cc)
    o_ref[...] = jnp.dot(s, w_ref[...], preferred_element_type=F32,
                         precision=lax.Precision.HIGHEST) + b_ref[...]


def _ada(cc, w_ada, b_ada):
    d, n = w_ada.shape
    tn = _tile(n, 1024)
    return pl.pallas_call(
        _ada_kernel,
        grid=(n // tn,),
        in_specs=[pl.BlockSpec((8, d), lambda j: (0, 0)),
                  pl.BlockSpec((d, tn), lambda j: (0, j)),
                  pl.BlockSpec((1, tn), lambda j: (0, j))],
        out_specs=pl.BlockSpec((8, tn), lambda j: (0, j)),
        out_shape=jax.ShapeDtypeStruct((8, n), F32),
        compiler_params=_params(("arbitrary",)),
        name="ada",
    )(cc, w_ada, b_ada.reshape(1, n))


def _inproj_kernel(x_ref, g_ref, sh_ref, sc_ref, w_ref, wvt_ref, cos_ref, sin_ref,
                   q_ref, k_ref, vt_ref, rx_ref, rz_ref, h_scr, *, q_scale):
    j = pl.program_id(1)
    tm = x_ref.shape[0]

    @pl.when(j == 0)
    def _():
        y = _rms(x_ref[...], EPS) * g_ref[...]
        h_scr[...] = (y * (1.0 + sc_ref[...]) + sh_ref[...]).astype(BF16)

    def proj():
        return jnp.dot(h_scr[...], w_ref[...], preferred_element_type=F32)

    def rope_into(o_ref, scale):
        acc = proj()
        cos = cos_ref[...]
        sin = sin_ref[...]
        lane = lax.broadcasted_iota(jnp.int32, (tm, LANES), 1)
        first = (lane % 32) < 16
        for g in range(acc.shape[1] // LANES):
            t = acc[:, g * LANES:(g + 1) * LANES]
            partner = jnp.where(first, pltpu.roll(t, LANES - 16, 1), pltpu.roll(t, 16, 1))
            r = t * cos + partner * sin
            if scale != 1.0:
                r = r * scale
            o_ref[:, g * LANES:(g + 1) * LANES] = r.astype(o_ref.dtype)

    @pl.when(j == 0)
    def _():
        rope_into(q_ref, q_scale)

    @pl.when(j == 1)
    def _():
        rope_into(k_ref, 1.0)

    @pl.when(j == 2)
    def _():
        vt_ref[...] = lax.dot_general(wvt_ref[...], h_scr[...], (((1,), (1,)), ((), ())),
                                      preferred_element_type=F32).astype(vt_ref.dtype)

    @pl.when(j == 3)
    def _():
        rx_ref[...] = proj()

    @pl.when(j == 4)
    def _():
        rz_ref[...] = proj()


def _inproj(x2, g, sh, sc, w_in, wvt, cos, sin, tm):
    s, d = x2.shape
    wcol = w_in.shape[1] // 5
    row = lambda i, j: (i, 0)
    vec = pl.BlockSpec((1, d), lambda i, j: (0, 0))
    out = lambda dt: jax.ShapeDtypeStruct((s, wcol), dt)
    return pl.pallas_call(
        functools.partial(_inproj_kernel, q_scale=HEAD_DIM ** -0.5 * LOG2E),
        grid=(s // tm, 5),
        in_specs=[pl.BlockSpec((tm, d), row), vec, vec, vec,
                  pl.BlockSpec((d, wcol), lambda i, j: (0, jnp.where(j == 2, 1, j))),
                  pl.BlockSpec(wvt.shape, lambda i, j: (0, 0)),
                  pl.BlockSpec((tm, LANES), row), pl.BlockSpec((tm, LANES), row)],
        out_specs=[pl.BlockSpec((tm, wcol), row), pl.BlockSpec((tm, wcol), row),
                   pl.BlockSpec((wcol, tm), lambda i, j: (0, i)),
                   pl.BlockSpec((tm, wcol), row), pl.BlockSpec((tm, wcol), row)],
        out_shape=[out(BF16), out(BF16), jax.ShapeDtypeStruct((wcol, s), BF16), out(F32), out(F32)],
        scratch_shapes=[pltpu.VMEM((tm, d), BF16)],
        compiler_params=_params(("arbitrary", "arbitrary")),
        name="inproj",
    )(x2, g, sh, sc, w_in, wvt, cos, sin)


TKC = MXU_N
QSUB = MXU_N
ONES_ROWS = 16
LOOKAHEAD = 2


def _attn_kernel(q_ref, k_ref, vt_ref, dl_ref, sg_ref, o_ref, vaug, *, n_chunks, unroll):
    tq = q_ref.shape[0]

    @pl.when(pl.program_id(1) == 0)
    def _():
        for c in range(n_chunks):
            vaug[c, :DV, :] = vt_ref[:, c * TKC:(c + 1) * TKC]
            vaug[c, DV:, :] = jnp.ones((ONES_ROWS, TKC), BF16)

    d = dl_ref[...]
    lam = (jnp.exp(jnp.sum(d[0:1] * d[1:2], axis=1, keepdims=True))
           - jnp.exp(jnp.sum(d[2:3] * d[3:4], axis=1, keepdims=True)) + LAM_INIT)

    def q_sub(qs, carry):
        r0 = pl.multiple_of(qs * QSUB, QSUB)
        qt = q_ref[pl.ds(r0, QSUB), :].astype(F32).T
        dim = lax.broadcasted_iota(jnp.int32, qt.shape, 0)
        qqt = jnp.concatenate([jnp.where(dim < HEAD_DIM, qt, 0.0),
                               jnp.where(dim >= HEAD_DIM, qt, 0.0)], axis=1).astype(BF16)

        def scores(c):
            c = jnp.minimum(c, n_chunks - 1)
            kc = k_ref[pl.ds(pl.multiple_of(c * TKC, TKC), TKC), :]
            return jnp.dot(kc, qqt, preferred_element_type=F32)

        def body(j, carry):
            m, acc = carry[:2]
            pending = list(carry[2:])
            for u in range(unroll):
                c = j * unroll + u
                pending.append(scores(c + LOOKAHEAD))
                st = pending.pop(0)
                m_new = jnp.maximum(m, jnp.max(st, axis=0, keepdims=True))
                pt = jnp.exp2(st - m_new).astype(BF16)
                pv = jnp.dot(vaug[c], pt, preferred_element_type=F32)
                acc = acc * jnp.exp2(m - m_new) + pv
                m = m_new
            return (m, acc, *pending)

        m0 = jnp.full((1, 2 * QSUB), NEG_BIG, F32)
        acc0 = jnp.zeros((DV + ONES_ROWS, 2 * QSUB), F32)
        first = [scores(jnp.int32(c)) for c in range(LOOKAHEAD)]
        acc = lax.fori_loop(0, n_chunks // unroll, body, (m0, acc0, *first))[1]

        r = acc[:DV, :] / acc[DV:DV + 1, :]
        ot = r[:, :QSUB] - lam * r[:, QSUB:]
        ot = ot * lax.rsqrt(jnp.mean(ot * ot, axis=0, keepdims=True) + SUBLN_EPS)
        o_ref[pl.ds(r0, QSUB), :] = (ot.T * sg_ref[...] * (1.0 - LAM_INIT)).astype(o_ref.dtype)
        return carry

    lax.fori_loop(0, tq // QSUB, q_sub, 0)


def _attn(q, k_all, vt_all, diff_lambda, subln_g, tq):
    s = q.shape[0]
    n_kv = k_all.shape[0]
    n_chunks = n_kv // TKC
    unroll = max(u for u in range(1, 14) if n_chunks % u == 0)
    return pl.pallas_call(
        functools.partial(_attn_kernel, n_chunks=n_chunks, unroll=unroll),
        grid=(N_HEADS, s // tq),
        in_specs=[pl.BlockSpec((tq, DV), lambda h, i: (i, h)),
                  pl.BlockSpec((n_kv, DV), lambda h, i: (0, h)),
                  pl.BlockSpec((DV, n_kv), lambda h, i: (h, 0)),
                  pl.BlockSpec((4, HEAD_DIM), lambda h, i: (0, 0)),
                  pl.BlockSpec((1, DV), lambda h, i: (0, 0))],
        out_specs=pl.BlockSpec((tq, DV), lambda h, i: (i, h)),
        out_shape=jax.ShapeDtypeStruct((s, N_HEADS * DV), BF16),
        scratch_shapes=[pltpu.VMEM((n_chunks, DV + ONES_ROWS, TKC), BF16)],
        compiler_params=_params(("arbitrary", "arbitrary")),
        name="attn",
    )(q, k_all, vt_all, diff_lambda, subln_g)


def _log1p(u):
    w = 1.0 + u
    return jnp.where(w == 1.0, u, jnp.log(w) * (u / (w - 1.0)))


def _softplus(z):
    return jnp.maximum(z, 0.0) + _log1p(jnp.exp(-jnp.abs(z)))


def _neg_expm1(u):
    e = jnp.exp(u)
    d = e - 1.0
    safe = jnp.where(d == 0.0, 1.0, jnp.log(e))
    return -jnp.where(d == 0.0, u, d * (u / safe))


def _rglru_kernel(cur_f, prev_f, next_f, cur_b, prev_b, next_b, cw_ref, cb_ref, w_ref,
                  ba_ref, bi_ref, lam_ref, h0_ref, hf_ref, hb_ref, hl_ref,
                  e_scr, a_f, b_f, a_b, b_b, h_scr, *, halo):
    i = pl.program_id(0)
    n = pl.num_programs(0)
    t_rows = cur_f.shape[0]

    @pl.when(i == 0)
    def _():
        h_scr[...] = h0_ref[...]

    def conv(cur, prev, nxt, first, last):
        e_scr[0:halo, :] = jnp.where(first, 0.0, prev[...])
        e_scr[halo:halo + t_rows, :] = cur[...]
        e_scr[halo + t_rows:, :] = jnp.where(last, 0.0, nxt[...])
        y = cb_ref[...] + cw_ref[0:1, :] * e_scr[halo - 2:halo - 2 + t_rows, :]
        y = y + cw_ref[1:2, :] * e_scr[halo - 1:halo - 1 + t_rows, :]
        y = y + cw_ref[2:3, :] * e_scr[halo:halo + t_rows, :]
        y = y + cw_ref[3:4, :] * e_scr[halo + 1:halo + 1 + t_rows, :]
        return y

    def gates(xr, d, a_scr, b_scr):
        xb = xr.astype(BF16)
        sp = _softplus(-lam_ref[d:d + 1, :])
        for k in range(N_REC_BLOCKS):
            sl = slice(k * REC_BLOCK, (k + 1) * REC_BLOCK)
            pre = jnp.dot(xb[:, sl], w_ref[d, k], preferred_element_type=F32)
            r = jax.nn.sigmoid(pre[:, :REC_BLOCK] + ba_ref[d:d + 1, sl])
            ig = jax.nn.sigmoid(pre[:, REC_BLOCK:] + bi_ref[d:d + 1, sl])
            log_a = (-RG_C) * r * sp[:, sl]
            a_scr[:, sl] = jnp.exp(log_a)
            b_scr[:, sl] = jnp.sqrt(_neg_expm1(2.0 * log_a)) * (ig * xr[:, sl])

    gates(conv(cur_f, prev_f, next_f, i == 0, i == n - 1), 0, a_f, b_f)
    gates(conv(cur_b, prev_b, next_b, i == n - 1, i == 0), 1, a_b, b_b)

    def step(t, carry):
        hf, hb = carry
        hf = a_f[pl.ds(t, 1), :] * hf + b_f[pl.ds(t, 1), :]
        hf_ref[pl.ds(t, 1), :] = hf
        tb = t_rows - 1 - t
        hb = a_b[pl.ds(tb, 1), :] * hb + b_b[pl.ds(tb, 1), :]
        hb_ref[pl.ds(tb, 1), :] = hb
        return hf, hb

    hf, hb = lax.fori_loop(0, t_rows, step, (h_scr[0:1, :], h_scr[1:2, :]), unroll=8)
    h_scr[0:1, :] = hf
    h_scr[1:2, :] = hb
    hl_ref[...] = h_scr[...]


def _rglru(rx, cw, cb, wcat, ba, bi, lam, h0, t_rows):
    s, c = rx.shape
    n = s // t_rows
    halo = 8
    bpc = t_rows // halo
    nhb = s // halo
    cur = lambda f: pl.BlockSpec((t_rows, c), lambda i: (f(i, n), 0))
    prv = lambda f: pl.BlockSpec((halo, c), lambda i: (jnp.maximum(f(i, n) * bpc - 1, 0), 0))
    nxt = lambda f: pl.BlockSpec((halo, c), lambda i: (jnp.minimum((f(i, n) + 1) * bpc, nhb - 1), 0))
    fwd = lambda i, n: i
    bwd = lambda i, n: n - 1 - i
    full = lambda shape: pl.BlockSpec(shape, lambda i: (0,) * len(shape))
    return pl.pallas_call(
        functools.partial(_rglru_kernel, halo=halo),
        grid=(n,),
        in_specs=[cur(fwd), prv(fwd), nxt(fwd), cur(bwd), prv(bwd), nxt(bwd),
                  full(cw.shape), full((1, c)), full(wcat.shape),
                  full((2, c)), full((2, c)), full((2, c)), full((8, c))],
        out_specs=[pl.BlockSpec((t_rows, c), lambda i: (i, 0)),
                   pl.BlockSpec((t_rows, c), lambda i: (n - 1 - i, 0)),
                   full((8, c))],
        out_shape=[jax.ShapeDtypeStruct((s, c), F32), jax.ShapeDtypeStruct((s, c), F32),
                   jax.ShapeDtypeStruct((8, c), F32)],
        scratch_shapes=[pltpu.VMEM((t_rows + 2 * halo, c), F32)]
                       + [pltpu.VMEM((t_rows, c), F32)] * 4
                       + [pltpu.VMEM((8, c), F32)],
        compiler_params=_params(("arbitrary",)),
        name="rglru",
    )(rx, rx, rx, rx, rx, rx, cw, cb.reshape(1, c), wcat, ba, bi, lam, h0)


def _outproj_kernel(attn_ref, hf_ref, hb_ref, rz_ref, x_ref, w_ref, g1_ref, n2_ref,
                    sh_ref, sc_ref, x1_ref, h2_ref):
    half = attn_ref.shape[1]
    rec = ((hf_ref[...] + hb_ref[...]) * _gelu_tanh(rz_ref[...])).astype(BF16)
    y = jnp.dot(attn_ref[...], w_ref[:half, :], preferred_element_type=F32)
    y = y + jnp.dot(rec, w_ref[half:, :], preferred_element_type=F32)
    x1 = x_ref[...] + g1_ref[...] * y
    x1_ref[...] = x1
    h2 = _rms(x1, EPS) * n2_ref[...]
    h2_ref[...] = (h2 * (1.0 + sc_ref[...]) + sh_ref[...]).astype(BF16)


def _outproj(attn, hf, hb, rz, x2, w_out, g1, n2, sh2, sc2, tm):
    s, d = x2.shape
    half = attn.shape[1]
    row = lambda i: (i, 0)
    vec = pl.BlockSpec((1, d), lambda i: (0, 0))
    return pl.pallas_call(
        _outproj_kernel,
        grid=(s // tm,),
        in_specs=[pl.BlockSpec((tm, half), row)] * 4
                 + [pl.BlockSpec((tm, d), row), pl.BlockSpec(w_out.shape, lambda i: (0, 0)),
                    vec, vec, vec, vec],
        out_specs=[pl.BlockSpec((tm, d), row), pl.BlockSpec((tm, d), row)],
        out_shape=[jax.ShapeDtypeStruct((s, d), F32), jax.ShapeDtypeStruct((s, d), BF16)],
        compiler_params=_params(("arbitrary",)),
        name="outproj",
    )(attn, hf, hb, rz, x2, w_out, g1, n2, sh2, sc2)


def _ffn_kernel(h_ref, hp_ref, hn_ref, x1_ref, wu_ref, wg_ref, wd_ref, cw_ref, cb_ref,
                g2_ref, fg_ref, o_ref, he, acc, *, halo):
    i = pl.program_id(0)
    f = pl.program_id(1)
    tm = h_ref.shape[0]

    @pl.when(f == 0)
    def _():
        zero = jnp.zeros(hp_ref.shape, BF16)
        he[0:halo, :] = jnp.where(i == 0, zero, hp_ref[...])
        he[halo:halo + tm, :] = h_ref[...]
        he[halo + tm:, :] = jnp.where(i == pl.num_programs(0) - 1, zero, hn_ref[...])
        acc[...] = jnp.zeros(acc.shape, F32)

    u = jnp.dot(h_ref[...], wu_ref[...], preferred_element_type=F32)
    g = jnp.dot(he[...], wg_ref[...], preferred_element_type=F32)
    rows = g.shape[0]
    gm = pltpu.roll(g, 1, 0)
    gp = pltpu.roll(g, rows - 1, 0)
    conv = cb_ref[...] + cw_ref[0:1, :] * gm + cw_ref[1:2, :] * g + cw_ref[2:3, :] * gp
    act = (_gelu_tanh(conv[halo:halo + tm, :]) * u).astype(BF16)
    acc[...] += jnp.dot(act, wd_ref[...], preferred_element_type=F32)

    @pl.when(f == pl.num_programs(1) - 1)
    def _():
        x2 = x1_ref[...] + g2_ref[...] * acc[...]
        o_ref[...] = _rms(x2, EPS) * fg_ref[...]


def _ffn(h2, x1, w_up, w_gate, w_down, cw, cb, g2, fg, tm, tf):
    s, d = h2.shape
    dff = w_up.shape[1]
    halo = 16
    bpc = tm // halo
    nhb = s // halo
    row = lambda i, f: (i, 0)
    vec = pl.BlockSpec((1, d), lambda i, f: (0, 0))
    return pl.pallas_call(
        functools.partial(_ffn_kernel, halo=halo),
        grid=(s // tm, dff // tf),
        in_specs=[pl.BlockSpec((tm, d), row),
                  pl.BlockSpec((halo, d), lambda i, f: (jnp.maximum(i * bpc - 1, 0), 0)),
                  pl.BlockSpec((halo, d), lambda i, f: (jnp.minimum((i + 1) * bpc, nhb - 1), 0)),
                  pl.BlockSpec((tm, d), row),
                  pl.BlockSpec((d, tf), lambda i, f: (0, f)),
                  pl.BlockSpec((d, tf), lambda i, f: (0, f)),
                  pl.BlockSpec((tf, d), lambda i, f: (f, 0)),
                  pl.BlockSpec((3, tf), lambda i, f: (0, f)),
                  pl.BlockSpec((1, tf), lambda i, f: (0, f)),
                  vec, vec],
        out_specs=pl.BlockSpec((tm, d), row),
        out_shape=jax.ShapeDtypeStruct((s, d), F32),
        scratch_shapes=[pltpu.VMEM((tm + 2 * halo, d), BF16), pltpu.VMEM((tm, d), F32)],
        compiler_params=_params(("arbitrary", "arbitrary")),
        name="ffn",
    )(h2, h2, h2, x1, w_up, w_gate, w_down, cw, cb, g2, fg)


def _rope_tables(s):
    pos = jnp.arange(s)
    inv = ROPE_THETA ** (-jnp.arange(N_FREQ, dtype=F32) / N_FREQ)
    ang_r = (pos // GRID_W).astype(F32)[:, None] * inv
    ang_c = (pos % GRID_W).astype(F32)[:, None] * inv
    cos = jnp.concatenate([jnp.cos(ang_r)] * 2 + [jnp.cos(ang_c)] * 2, axis=1)
    sin = jnp.concatenate([-jnp.sin(ang_r), jnp.sin(ang_r), -jnp.sin(ang_c), jnp.sin(ang_c)], axis=1)
    return jnp.tile(cos, (1, DV // HEAD_DIM)), jnp.tile(sin, (1, DV // HEAD_DIM))


def kernel(x, c, ctx, c_ctx, w_ada, b_ada, norm1_g, w_in, rec_conv_w, rec_conv_b, rg_wa, rg_ba,
           rg_wi, rg_bi, rg_lambda, diff_lambda, subln_g, w_out, norm2_g, w_up, w_gate,
           ffn_conv_w, ffn_conv_b, w_down, final_g):
    assert x.shape[0] == 1 and w_ada.shape[0] == 1, "single batch element, single layer"
    s, d = x.shape[1:]
    n_ctx = ctx.shape[1]
    x2 = x[0]
    cx2 = ctx[0]

    cc = jnp.zeros((8, d), F32).at[0].set(c[0]).at[1].set(c_ctx)
    mod = _ada(cc, w_ada[0], b_ada[0])
    sh1, sc1, g1, sh2, sc2, g2 = [mod[0:1, j * d:(j + 1) * d] for j in range(6)]
    csh1, csc1 = mod[1:2, 0:d], mod[1:2, d:2 * d]

    w_in_b = w_in[0].astype(BF16)
    wcol = w_in_b.shape[1] // 5
    wvt = w_in_b[:, 2 * wcol:3 * wcol].T
    n1 = norm1_g[0].reshape(1, d)
    cos, sin = _rope_tables(s)
    q, k, vt, rx, rz = _inproj(x2, n1, sh1, sc1, w_in_b, wvt, cos, sin, _tile(s, 512))
    _, kc, vtc, rxc, _ = _inproj(cx2, n1, csh1, csc1, w_in_b, wvt, jnp.ones((n_ctx, DV), F32),
                                 jnp.zeros((n_ctx, DV), F32), n_ctx)

    k_all = jnp.concatenate([kc, k], axis=0)
    vt_all = jnp.concatenate([vtc, vt], axis=1)
    attn = _attn(q, k_all, vt_all, diff_lambda[0], subln_g[0].reshape(1, DV), _tile(s, 512))

    wcat = jnp.concatenate([rg_wa[0], rg_wi[0]], axis=-1).astype(BF16)
    rec_args = (rec_conv_w[0], rec_conv_b[0], wcat, rg_ba[0], rg_bi[0], rg_lambda[0])
    _, _, h_ctx = _rglru(rxc, *rec_args, jnp.zeros((8, rxc.shape[1]), F32), n_ctx)
    hf, hb, _ = _rglru(rx, *rec_args, h_ctx, _tile(s, 256))

    x1, h2 = _outproj(attn, hf, hb, rz, x2, w_out[0].astype(BF16), g1,
                      norm2_g[0].reshape(1, d), sh2, sc2, _tile(s, 512))

    dff = w_up.shape[2]
    tf = 512
    pad = (-dff) % tf
    w_up_b = jnp.pad(w_up[0].astype(BF16), ((0, 0), (0, pad)))
    w_gate_b = jnp.pad(w_gate[0].astype(BF16), ((0, 0), (0, pad)))
    w_down_b = jnp.pad(w_down[0].astype(BF16), ((0, pad), (0, 0)))
    cw = jnp.pad(ffn_conv_w[0], ((0, 0), (0, pad)))
    cb = jnp.pad(ffn_conv_b[0], ((0, pad),)).reshape(1, dff + pad)
    out = _ffn(h2, x1, w_up_b, w_gate_b, w_down_b, cw, cb, g2, final_g.reshape(1, d),
               _tile(s, 512), tf)
    return out[None]
```

```python
import functools
import math

import jax
import jax.numpy as jnp
from jax import lax
from jax.experimental import pallas as pl
from jax.experimental.pallas import tpu as pltpu

F32 = jnp.float32
BF16 = jnp.bfloat16

GRID_W = 64
N_HEADS = 8
DV = 128
HEAD_DIM = DV // 2
N_FREQ = HEAD_DIM // 4
N_REC_BLOCKS = 8
REC_BLOCK = 128
RG_C = 8.0
ROPE_THETA = 10000.0
EPS = 1e-6
SUBLN_EPS = 1e-5
LAM_INIT = 0.8 - 0.6 * math.exp(-0.3 * 0)
LOG2E = 1.4426950408889634
NEG_BIG = -1e30

LANES = 128
MXU_N = 256
VMEM_LIMIT = 56 * 1024 * 1024


def _params(sem, vmem=VMEM_LIMIT):
    return pltpu.CompilerParams(dimension_semantics=sem, vmem_limit_bytes=vmem)


def _tile(n, pref):
    if n <= pref:
        return n
    t = pref
    while t >= 16:
        if n % t == 0:
            return t
        t -= 16
    return n


def _gelu_tanh(x):
    return 0.5 * x * (1.0 + jnp.tanh(math.sqrt(2.0 / math.pi) * (x + 0.044715 * (x * x * x))))


def _rms(x, eps):
    return x * lax.rsqrt(jnp.mean(x * x, axis=-1, keepdims=True) + eps)


def _ada_kernel(c_ref, w_ref, b_ref, o_ref):
    cc = c_ref[...]
    s = cc * jax.nn.sigmoid(cc)
    o_ref[...] = jnp.dot(s, w_ref[...], preferred_element_type=F32,
                         precision=lax.Precision.HIGHEST) + b_ref[...]


def _ada(cc, w_ada, b_ada):
    d, n = w_ada.shape
    tn = _tile(n, 1024)
    return pl.pallas_call(
        _ada_kernel,
        grid=(n // tn,),
        in_specs=[pl.BlockSpec((8, d), lambda j: (0, 0)),
                  pl.BlockSpec((d, tn), lambda j: (0, j)),
                  pl.BlockSpec((1, tn), lambda j: (0, j))],
        out_specs=pl.BlockSpec((8, tn), lambda j: (0, j)),
        out_shape=jax.ShapeDtypeStruct((8, n), F32),
        compiler_params=_params(("arbitrary",)),
        name="ada",
    )(cc, w_ada, b_ada.reshape(1, n))


def _inproj_kernel(x_ref, g_ref, sh_ref, sc_ref, w_ref, wvt_ref, cos_ref, sin_ref,
                   q_ref, k_ref, vt_ref, rx_ref, rz_ref, h_scr, *, q_scale):
    j = pl.program_id(1)
    tm = x_ref.shape[0]

    @pl.when(j == 0)
    def _():
        y = _rms(x_ref[...], EPS) * g_ref[...]
        h_scr[...] = (y * (1.0 + sc_ref[...]) + sh_ref[...]).astype(BF16)

    def proj():
        return jnp.dot(h_scr[...], w_ref[...], preferred_element_type=F32)

    def rope_into(o_ref, scale):
        acc = proj()
        cos = cos_ref[...]
        sin = sin_ref[...]
        lane = lax.broadcasted_iota(jnp.int32, (tm, LANES), 1)
        first = (lane % 32) < 16
        for g in range(acc.shape[1] // LANES):
            t = acc[:, g * LANES:(g + 1) * LANES]
            partner = jnp.where(first, pltpu.roll(t, LANES - 16, 1), pltpu.roll(t, 16, 1))
            r = t * cos + partner * sin
            if scale != 1.0:
                r = r * scale
            o_ref[:, g * LANES:(g + 1) * LANES] = r.astype(o_ref.dtype)

    @pl.when(j == 0)
    def _():
        rope_into(q_ref, q_scale)

    @pl.when(j == 1)
    def _():
        rope_into(k_ref, 1.0)

    @pl.when(j == 2)
    def _():
        vt_ref[...] = lax.dot_general(wvt_ref[...], h_scr[...], (((1,), (1,)), ((), ())),
                                      preferred_element_type=F32).astype(vt_ref.dtype)

    @pl.when(j == 3)
    def _():
        rx_ref[...] = proj()

    @pl.when(j == 4)
    def _():
        rz_ref[...] = proj()


def _inproj(x2, g, sh, sc, w_in, wvt, cos, sin, tm):
    s, d = x2.shape
    wcol = w_in.shape[1] // 5
    row = lambda i, j: (i, 0)
    vec = pl.BlockSpec((1, d), lambda i, j: (0, 0))
    out = lambda dt: jax.ShapeDtypeStruct((s, wcol), dt)
    return pl.pallas_call(
        functools.partial(_inproj_kernel, q_scale=HEAD_DIM ** -0.5 * LOG2E),
        grid=(s // tm, 5),
        in_specs=[pl.BlockSpec((tm, d), row), vec, vec, vec,
                  pl.BlockSpec((d, wcol), lambda i, j: (0, jnp.where(j == 2, 1, j))),
                  pl.BlockSpec(wvt.shape, lambda i, j: (0, 0)),
                  pl.BlockSpec((tm, LANES), row), pl.BlockSpec((tm, LANES), row)],
        out_specs=[pl.BlockSpec((tm, wcol), row), pl.BlockSpec((tm, wcol), row),
                   pl.BlockSpec((wcol, tm), lambda i, j: (0, i)),
                   pl.BlockSpec((tm, wcol), row), pl.BlockSpec((tm, wcol), row)],
        out_shape=[out(BF16), out(BF16), jax.ShapeDtypeStruct((wcol, s), BF16), out(F32), out(F32)],
        scratch_shapes=[pltpu.VMEM((tm, d), BF16)],
        compiler_params=_params(("arbitrary", "arbitrary")),
        name="inproj",
    )(x2, g, sh, sc, w_in, wvt, cos, sin)


TKC = MXU_N
QSUB = MXU_N
ONES_ROWS = 16
LOOKAHEAD = 2
GROUP = 1


def _attn_kernel(q_ref, kc_ref, k_ref, vtc_ref, vt_ref, dl_ref, sg_ref, o_ref, vaug):
    tq = q_ref.shape[0]
    chunks = ([(kc_ref, vtc_ref, o) for o in range(0, kc_ref.shape[0], TKC)]
              + [(k_ref, vt_ref, o) for o in range(0, k_ref.shape[0], TKC)])
    n_chunks = len(chunks)
    lead = n_chunks % GROUP
    groups = ([list(range(lead))] if lead else []) + [
        list(range(c, c + GROUP)) for c in range(lead, n_chunks, GROUP)]

    @pl.when(pl.program_id(1) == 0)
    def _():
        for c, (_, v_src, o) in enumerate(chunks):
            vaug[c, :DV, :] = v_src[:, o:o + TKC]
            vaug[c, DV:, :] = jnp.ones((ONES_ROWS, TKC), BF16)

    d = dl_ref[...]
    lam = (jnp.exp(jnp.sum(d[0:1] * d[1:2], axis=1, keepdims=True))
           - jnp.exp(jnp.sum(d[2:3] * d[3:4], axis=1, keepdims=True)) + LAM_INIT)

    def q_sub(qs, carry):
        r0 = pl.multiple_of(qs * QSUB, QSUB)
        qt = q_ref[pl.ds(r0, QSUB), :].astype(F32).T
        dim = lax.broadcasted_iota(jnp.int32, qt.shape, 0)
        qqt = jnp.concatenate([jnp.where(dim < HEAD_DIM, qt, 0.0),
                               jnp.where(dim >= HEAD_DIM, qt, 0.0)], axis=1).astype(BF16)

        def scores(c):
            k_src, _, o = chunks[c]
            return jnp.dot(k_src[o:o + TKC, :], qqt, preferred_element_type=F32)

        m = jnp.full((1, 2 * QSUB), NEG_BIG, F32)
        acc = jnp.zeros((DV + ONES_ROWS, 2 * QSUB), F32)
        pending = {}
        issued = 0
        for grp in groups:
            while issued < min(grp[-1] + LOOKAHEAD + 1, n_chunks):
                pending[issued] = scores(issued)
                issued += 1
            sts = [pending.pop(c) for c in grp]
            m_new = m
            for st in sts:
                m_new = jnp.maximum(m_new, jnp.max(st, axis=0, keepdims=True))
            pv = None
            for c, st in zip(grp, sts):
                pt = jnp.exp2(st - m_new).astype(BF16)
                part = jnp.dot(vaug[c], pt, preferred_element_type=F32)
                pv = part if pv is None else pv + part
            acc = acc * jnp.exp2(m - m_new) + pv
            m = m_new

        r = acc[:DV, :] / acc[DV:DV + 1, :]
        ot = r[:, :QSUB] - lam * r[:, QSUB:]
        ot = ot * lax.rsqrt(jnp.mean(ot * ot, axis=0, keepdims=True) + SUBLN_EPS)
        o_ref[pl.ds(r0, QSUB), :] = (ot.T * sg_ref[...] * (1.0 - LAM_INIT)).astype(o_ref.dtype)
        return carry

    lax.fori_loop(0, tq // QSUB, q_sub, 0)


def _attn(q, kc, k, vtc, vt, diff_lambda, subln_g, tq):
    s = q.shape[0]
    n_ctx = kc.shape[0]
    assert n_ctx % TKC == 0 and s % TKC == 0 and tq % QSUB == 0
    head_rows = lambda n: pl.BlockSpec((n, DV), lambda h, i: (0, h))
    head_cols = lambda n: pl.BlockSpec((DV, n), lambda h, i: (h, 0))
    return pl.pallas_call(
        _attn_kernel,
        grid=(N_HEADS, s // tq),
        in_specs=[pl.BlockSpec((tq, DV), lambda h, i: (i, h)),
                  head_rows(n_ctx), head_rows(s), head_cols(n_ctx), head_cols(s),
                  pl.BlockSpec((4, HEAD_DIM), lambda h, i: (0, 0)),
                  pl.BlockSpec((1, DV), lambda h, i: (0, 0))],
        out_specs=pl.BlockSpec((tq, DV), lambda h, i: (i, h)),
        out_shape=jax.ShapeDtypeStruct((s, N_HEADS * DV), BF16),
        scratch_shapes=[pltpu.VMEM(((n_ctx + s) // TKC, DV + ONES_ROWS, TKC), BF16)],
        compiler_params=_params(("arbitrary", "arbitrary")),
        name="attn",
    )(q, kc, k, vtc, vt, diff_lambda, subln_g)


def _log1p(u):
    w = 1.0 + u
    return jnp.where(w == 1.0, u, jnp.log(w) * (u / (w - 1.0)))


def _softplus(z):
    return jnp.maximum(z, 0.0) + _log1p(jnp.exp(-jnp.abs(z)))


def _neg_expm1(u):
    e = jnp.exp(u)
    d = e - 1.0
    safe = jnp.where(d == 0.0, 1.0, jnp.log(e))
    return -jnp.where(d == 0.0, u, d * (u / safe))


def _rglru_kernel(cur_f, prev_f, next_f, cur_b, prev_b, next_b, cw_ref, cb_ref, w_ref,
                  ba_ref, bi_ref, lam_ref, h0_ref, hf_ref, hb_ref, hl_ref,
                  e_scr, a_f, b_f, a_b, b_b, h_scr, *, halo):
    i = pl.program_id(0)
    n = pl.num_programs(0)
    t_rows = cur_f.shape[0]

    @pl.when(i == 0)
    def _():
        h_scr[...] = h0_ref[...]

    def conv(cur, prev, nxt, first, last):
        e_scr[0:halo, :] = jnp.where(first, 0.0, prev[...])
        e_scr[halo:halo + t_rows, :] = cur[...]
        e_scr[halo + t_rows:, :] = jnp.where(last, 0.0, nxt[...])
        y = cb_ref[...] + cw_ref[0:1, :] * e_scr[halo - 2:halo - 2 + t_rows, :]
        y = y + cw_ref[1:2, :] * e_scr[halo - 1:halo - 1 + t_rows, :]
        y = y + cw_ref[2:3, :] * e_scr[halo:halo + t_rows, :]
        y = y + cw_ref[3:4, :] * e_scr[halo + 1:halo + 1 + t_rows, :]
        return y

    def gates(xr, d, a_scr, b_scr):
        xb = xr.astype(BF16)
        sp = _softplus(-lam_ref[d:d + 1, :])
        for k in range(N_REC_BLOCKS):
            sl = slice(k * REC_BLOCK, (k + 1) * REC_BLOCK)
            pre = jnp.dot(xb[:, sl], w_ref[d, k], preferred_element_type=F32)
            r = jax.nn.sigmoid(pre[:, :REC_BLOCK] + ba_ref[d:d + 1, sl])
            ig = jax.nn.sigmoid(pre[:, REC_BLOCK:] + bi_ref[d:d + 1, sl])
            log_a = (-RG_C) * r * sp[:, sl]
            a_scr[:, sl] = jnp.exp(log_a)
            b_scr[:, sl] = jnp.sqrt(_neg_expm1(2.0 * log_a)) * (ig * xr[:, sl])

    gates(conv(cur_f, prev_f, next_f, i == 0, i == n - 1), 0, a_f, b_f)
    gates(conv(cur_b, prev_b, next_b, i == n - 1, i == 0), 1, a_b, b_b)

    def step(t, carry):
        hf, hb = carry
        hf = a_f[pl.ds(t, 1), :] * hf + b_f[pl.ds(t, 1), :]
        hf_ref[pl.ds(t, 1), :] = hf
        tb = t_rows - 1 - t
        hb = a_b[pl.ds(tb, 1), :] * hb + b_b[pl.ds(tb, 1), :]
        hb_ref[pl.ds(tb, 1), :] = hb
        return hf, hb

    hf, hb = lax.fori_loop(0, t_rows, step, (h_scr[0:1, :], h_scr[1:2, :]), unroll=8)
    h_scr[0:1, :] = hf
    h_scr[1:2, :] = hb
    hl_ref[...] = h_scr[...]


def _rglru(rx, cw, cb, wcat, ba, bi, lam, h0, t_rows):
    s, c = rx.shape
    n = s // t_rows
    halo = 8
    bpc = t_rows // halo
    nhb = s // halo
    cur = lambda f: pl.BlockSpec((t_rows, c), lambda i: (f(i, n), 0))
    prv = lambda f: pl.BlockSpec((halo, c), lambda i: (jnp.maximum(f(i, n) * bpc - 1, 0), 0))
    nxt = lambda f: pl.BlockSpec((halo, c), lambda i: (jnp.minimum((f(i, n) + 1) * bpc, nhb - 1), 0))
    fwd = lambda i, n: i
    bwd = lambda i, n: n - 1 - i
    full = lambda shape: pl.BlockSpec(shape, lambda i: (0,) * len(shape))
    return pl.pallas_call(
        functools.partial(_rglru_kernel, halo=halo),
        grid=(n,),
        in_specs=[cur(fwd), prv(fwd), nxt(fwd), cur(bwd), prv(bwd), nxt(bwd),
                  full(cw.shape), full((1, c)), full(wcat.shape),
                  full((2, c)), full((2, c)), full((2, c)), full((8, c))],
        out_specs=[pl.BlockSpec((t_rows, c), lambda i: (i, 0)),
                   pl.BlockSpec((t_rows, c), lambda i: (n - 1 - i, 0)),
                   full((8, c))],
        out_shape=[jax.ShapeDtypeStruct((s, c), F32), jax.ShapeDtypeStruct((s, c), F32),
                   jax.ShapeDtypeStruct((8, c), F32)],
        scratch_shapes=[pltpu.VMEM((t_rows + 2 * halo, c), F32)]
                       + [pltpu.VMEM((t_rows, c), F32)] * 4
                       + [pltpu.VMEM((8, c), F32)],
        compiler_params=_params(("arbitrary",)),
        name="rglru",
    )(rx, rx, rx, rx, rx, rx, cw, cb.reshape(1, c), wcat, ba, bi, lam, h0)


def _outproj_kernel(attn_ref, hf_ref, hb_ref, rz_ref, x_ref, w_ref, g1_ref, n2_ref,
                    sh_ref, sc_ref, x1_ref, h2_ref):
    half = attn_ref.shape[1]
    rec = ((hf_ref[...] + hb_ref[...]) * _gelu_tanh(rz_ref[...])).astype(BF16)
    y = jnp.dot(attn_ref[...], w_ref[:half, :], preferred_element_type=F32)
    y = y + jnp.dot(rec, w_ref[half:, :], preferred_element_type=F32)
    x1 = x_ref[...] + g1_ref[...] * y
    x1_ref[...] = x1
    h2 = _rms(x1, EPS) * n2_ref[...]
    h2_ref[...] = (h2 * (1.0 + sc_ref[...]) + sh_ref[...]).astype(BF16)


def _outproj(attn, hf, hb, rz, x2, w_out, g1, n2, sh2, sc2, tm):
    s, d = x2.shape
    half = attn.shape[1]
    row = lambda i: (i, 0)
    vec = pl.BlockSpec((1, d), lambda i: (0, 0))
    return pl.pallas_call(
        _outproj_kernel,
        grid=(s // tm,),
        in_specs=[pl.BlockSpec((tm, half), row)] * 4
                 + [pl.BlockSpec((tm, d), row), pl.BlockSpec(w_out.shape, lambda i: (0, 0)),
                    vec, vec, vec, vec],
        out_specs=[pl.BlockSpec((tm, d), row), pl.BlockSpec((tm, d), row)],
        out_shape=[jax.ShapeDtypeStruct((s, d), F32), jax.ShapeDtypeStruct((s, d), BF16)],
        compiler_params=_params(("arbitrary",)),
        name="outproj",
    )(attn, hf, hb, rz, x2, w_out, g1, n2, sh2, sc2)


def _ffn_kernel(h_ref, hp_ref, hn_ref, x1_ref, wu_ref, wg_ref, wd_ref, cw_ref, cb_ref,
                g2_ref, fg_ref, o_ref, he, acc, *, halo):
    i = pl.program_id(0)
    f = pl.program_id(1)
    tm = h_ref.shape[0]

    @pl.when(f == 0)
    def _():
        zero = jnp.zeros(hp_ref.shape, BF16)
        he[0:halo, :] = jnp.where(i == 0, zero, hp_ref[...])
        he[halo:halo + tm, :] = h_ref[...]
        he[halo + tm:, :] = jnp.where(i == pl.num_programs(0) - 1, zero, hn_ref[...])
        acc[...] = jnp.zeros(acc.shape, F32)

    u = jnp.dot(h_ref[...], wu_ref[...], preferred_element_type=F32)
    g = jnp.dot(he[...], wg_ref[...], preferred_element_type=F32)
    rows = g.shape[0]
    gm = pltpu.roll(g, 1, 0)
    gp = pltpu.roll(g, rows - 1, 0)
    conv = cb_ref[...] + cw_ref[0:1, :] * gm + cw_ref[1:2, :] * g + cw_ref[2:3, :] * gp
    act = (_gelu_tanh(conv[halo:halo + tm, :]) * u).astype(BF16)
    acc[...] += jnp.dot(act, wd_ref[...], preferred_element_type=F32)

    @pl.when(f == pl.num_programs(1) - 1)
    def _():
        x2 = x1_ref[...] + g2_ref[...] * acc[...]
        o_ref[...] = _rms(x2, EPS) * fg_ref[...]


def _ffn(h2, x1, w_up, w_gate, w_down, cw, cb, g2, fg, tm, tf):
    s, d = h2.shape
    dff = w_up.shape[1]
    halo = 16
    bpc = tm // halo
    nhb = s // halo
    row = lambda i, f: (i, 0)
    vec = pl.BlockSpec((1, d), lambda i, f: (0, 0))
    return pl.pallas_call(
        functools.partial(_ffn_kernel, halo=halo),
        grid=(s // tm, dff // tf),
        in_specs=[pl.BlockSpec((tm, d), row),
                  pl.BlockSpec((halo, d), lambda i, f: (jnp.maximum(i * bpc - 1, 0), 0)),
                  pl.BlockSpec((halo, d), lambda i, f: (jnp.minimum((i + 1) * bpc, nhb - 1), 0)),
                  pl.BlockSpec((tm, d), row),
                  pl.BlockSpec((d, tf), lambda i, f: (0, f)),
                  pl.BlockSpec((d, tf), lambda i, f: (0, f)),
                  pl.BlockSpec((tf, d), lambda i, f: (f, 0)),
                  pl.BlockSpec((3, tf), lambda i, f: (0, f)),
                  pl.BlockSpec((1, tf), lambda i, f: (0, f)),
                  vec, vec],
        out_specs=pl.BlockSpec((tm, d), row),
        out_shape=jax.ShapeDtypeStruct((s, d), F32),
        scratch_shapes=[pltpu.VMEM((tm + 2 * halo, d), BF16), pltpu.VMEM((tm, d), F32)],
        compiler_params=_params(("arbitrary", "arbitrary")),
        name="ffn",
    )(h2, h2, h2, x1, w_up, w_gate, w_down, cw, cb, g2, fg)


def _rope_tables(s):
    pos = jnp.arange(s)
    inv = ROPE_THETA ** (-jnp.arange(N_FREQ, dtype=F32) / N_FREQ)
    ang_r = (pos // GRID_W).astype(F32)[:, None] * inv
    ang_c = (pos % GRID_W).astype(F32)[:, None] * inv
    cos = jnp.concatenate([jnp.cos(ang_r)] * 2 + [jnp.cos(ang_c)] * 2, axis=1)
    sin = jnp.concatenate([-jnp.sin(ang_r), jnp.sin(ang_r), -jnp.sin(ang_c), jnp.sin(ang_c)], axis=1)
    return jnp.tile(cos, (1, DV // HEAD_DIM)), jnp.tile(sin, (1, DV // HEAD_DIM))


def kernel(x, c, ctx, c_ctx, w_ada, b_ada, norm1_g, w_in, rec_conv_w, rec_conv_b, rg_wa, rg_ba,
           rg_wi, rg_bi, rg_lambda, diff_lambda, subln_g, w_out, norm2_g, w_up, w_gate,
           ffn_conv_w, ffn_conv_b, w_down, final_g):
    assert x.shape[0] == 1 and w_ada.shape[0] == 1, "single batch element, single layer"
    s, d = x.shape[1:]
    n_ctx = ctx.shape[1]
    x2 = x[0]
    cx2 = ctx[0]

    cc = jnp.zeros((8, d), F32).at[0].set(c[0]).at[1].set(c_ctx)
    mod = _ada(cc, w_ada[0], b_ada[0])
    sh1, sc1, g1, sh2, sc2, g2 = [mod[0:1, j * d:(j + 1) * d] for j in range(6)]
    csh1, csc1 = mod[1:2, 0:d], mod[1:2, d:2 * d]

    w_in_b = w_in[0].astype(BF16)
    wcol = w_in_b.shape[1] // 5
    wvt = w_in_b[:, 2 * wcol:3 * wcol].T
    n1 = norm1_g[0].reshape(1, d)
    cos, sin = _rope_tables(s)
    q, k, vt, rx, rz = _inproj(x2, n1, sh1, sc1, w_in_b, wvt, cos, sin, _tile(s, 512))
    _, kc, vtc, rxc, _ = _inproj(cx2, n1, csh1, csc1, w_in_b, wvt, jnp.ones((n_ctx, DV), F32),
                                 jnp.zeros((n_ctx, DV), F32), n_ctx)

    attn = _attn(q, kc, k, vtc, vt, diff_lambda[0], subln_g[0].reshape(1, DV), _tile(s, 1024))

    wcat = jnp.concatenate([rg_wa[0], rg_wi[0]], axis=-1).astype(BF16)
    rec_args = (rec_conv_w[0], rec_conv_b[0], wcat, rg_ba[0], rg_bi[0], rg_lambda[0])
    _, _, h_ctx = _rglru(rxc, *rec_args, jnp.zeros((8, rxc.shape[1]), F32), n_ctx)
    hf, hb, _ = _rglru(rx, *rec_args, h_ctx, _tile(s, 256))

    x1, h2 = _outproj(attn, hf, hb, rz, x2, w_out[0].astype(BF16), g1,
                      norm2_g[0].reshape(1, d), sh2, sc2, _tile(s, 512))

    dff = w_up.shape[2]
    tf = 512
    pad = (-dff) % tf
    w_up_b = jnp.pad(w_up[0], ((0, 0), (0, pad))).astype(BF16)
    w_gate_b = jnp.pad(w_gate[0], ((0, 0), (0, pad))).astype(BF16)
    w_down_b = jnp.pad(w_down[0], ((0, pad), (0, 0))).astype(BF16)
    cw = jnp.pad(ffn_conv_w[0], ((0, 0), (0, pad)))
    cb = jnp.pad(ffn_conv_b[0], ((0, pad),)).reshape(1, dff + pad)
    out = _ffn(h2, x1, w_up_b, w_gate_b, w_down_b, cw, cb, g2, final_g.reshape(1, d),
               _tile(s, 512), tf)
    return out[None]
```

```python
import functools
import math

import jax
import jax.numpy as jnp
from jax import lax
from jax.experimental import pallas as pl
from jax.experimental.pallas import tpu as pltpu

F32 = jnp.float32
BF16 = jnp.bfloat16

GRID_W = 64
N_HEADS = 8
DV = 128
HEAD_DIM = DV // 2
N_FREQ = HEAD_DIM // 4
N_REC_BLOCKS = 8
REC_BLOCK = 128
RG_C = 8.0
ROPE_THETA = 10000.0
EPS = 1e-6
SUBLN_EPS = 1e-5
LAM_INIT = 0.8 - 0.6 * math.exp(-0.3 * 0)
LOG2E = 1.4426950408889634
NEG_BIG = -1e30

LANES = 128
SUBLANES = 8
MXU_N = 256
VMEM_LIMIT = 56 * 1024 * 1024


def _params(sem, vmem=VMEM_LIMIT):
    return pltpu.CompilerParams(dimension_semantics=sem, vmem_limit_bytes=vmem)


def _tile(n, pref):
    if n <= pref:
        return n
    t = pref
    while t >= 16:
        if n % t == 0:
            return t
        t -= 16
    return n


def _gelu_tanh(x):
    k0 = -2.0 * math.sqrt(2.0 / math.pi) * LOG2E
    return x / (1.0 + jnp.exp2(x * (k0 + (k0 * 0.044715) * (x * x))))


def _rms(x, eps):
    return x * lax.rsqrt(jnp.mean(x * x, axis=-1, keepdims=True) + eps)


def _ada_kernel(c_ref, w_ref, b_ref, o_ref):
    cc = c_ref[...]
    s = cc * jax.nn.sigmoid(cc)
    o_ref[...] = jnp.dot(s, w_ref[...], preferred_element_type=F32,
                         precision=lax.Precision.HIGHEST) + b_ref[...]


def _ada(cc, w_ada, b_ada):
    d, n = w_ada.shape
    tn = _tile(n, 1024)
    return pl.pallas_call(
        _ada_kernel,
        grid=(n // tn,),
        in_specs=[pl.BlockSpec((8, d), lambda j: (0, 0)),
                  pl.BlockSpec((d, tn), lambda j: (0, j)),
                  pl.BlockSpec((1, tn), lambda j: (0, j))],
        out_specs=pl.BlockSpec((8, tn), lambda j: (0, j)),
        out_shape=jax.ShapeDtypeStruct((8, n), F32),
        compiler_params=_params(("arbitrary",)),
        name="ada",
    )(cc, w_ada, b_ada.reshape(1, n))


def _inproj_kernel(x_ref, g_ref, sh_ref, sc_ref, w_ref, wvt_ref, cos_ref, sin_ref,
                   q_ref, k_ref, vt_ref, rx_ref, rz_ref, h_scr, *, q_scale):
    j = pl.program_id(1)
    tm = x_ref.shape[0]

    @pl.when(j == 0)
    def _():
        y = _rms(x_ref[...], EPS) * g_ref[...]
        h_scr[...] = (y * (1.0 + sc_ref[...]) + sh_ref[...]).astype(BF16)

    def proj():
        return jnp.dot(h_scr[...], w_ref[...], preferred_element_type=F32)

    def rope_into(o_ref, scale):
        acc = proj()
        cos = cos_ref[...]
        sin = sin_ref[...]
        lane = lax.broadcasted_iota(jnp.int32, (tm, LANES), 1)
        first = (lane % 32) < 16
        for g in range(acc.shape[1] // LANES):
            t = acc[:, g * LANES:(g + 1) * LANES]
            partner = jnp.where(first, pltpu.roll(t, LANES - 16, 1), pltpu.roll(t, 16, 1))
            r = t * cos + partner * sin
            if scale != 1.0:
                r = r * scale
            o_ref[:, g * LANES:(g + 1) * LANES] = r.astype(o_ref.dtype)

    @pl.when(j == 0)
    def _():
        rope_into(q_ref, q_scale)

    @pl.when(j == 1)
    def _():
        rope_into(k_ref, 1.0)

    @pl.when(j == 2)
    def _():
        vt_ref[...] = lax.dot_general(wvt_ref[...], h_scr[...], (((1,), (1,)), ((), ())),
                                      preferred_element_type=F32).astype(vt_ref.dtype)

    @pl.when(j == 3)
    def _():
        rx_ref[...] = proj()

    @pl.when(j == 4)
    def _():
        rz_ref[...] = proj()


def _inproj(x2, g, sh, sc, w_in, wvt, cos, sin, tm):
    s, d = x2.shape
    wcol = w_in.shape[1] // 5
    row = lambda i, j: (i, 0)
    vec = pl.BlockSpec((1, d), lambda i, j: (0, 0))
    out = lambda dt: jax.ShapeDtypeStruct((s, wcol), dt)
    return pl.pallas_call(
        functools.partial(_inproj_kernel, q_scale=HEAD_DIM ** -0.5 * LOG2E),
        grid=(s // tm, 5),
        in_specs=[pl.BlockSpec((tm, d), row), vec, vec, vec,
                  pl.BlockSpec((d, wcol), lambda i, j: (0, jnp.where(j == 2, 1, j))),
                  pl.BlockSpec(wvt.shape, lambda i, j: (0, 0)),
                  pl.BlockSpec((tm, LANES), row), pl.BlockSpec((tm, LANES), row)],
        out_specs=[pl.BlockSpec((tm, wcol), row), pl.BlockSpec((tm, wcol), row),
                   pl.BlockSpec((wcol, tm), lambda i, j: (0, i)),
                   pl.BlockSpec((tm, wcol), row), pl.BlockSpec((tm, wcol), row)],
        out_shape=[out(BF16), out(BF16), jax.ShapeDtypeStruct((wcol, s), BF16), out(F32), out(F32)],
        scratch_shapes=[pltpu.VMEM((tm, d), BF16)],
        compiler_params=_params(("arbitrary", "arbitrary")),
        name="inproj",
    )(x2, g, sh, sc, w_in, wvt, cos, sin)


TKC = MXU_N
QSUB = MXU_N
LOOKAHEAD = 2
OVERFLOW_GUARD = 2.0 ** 100


def _attn_kernel(q_ref, kc_ref, k_ref, vtc_ref, vt_ref, dl_ref, sg_ref, o_ref):
    tq = q_ref.shape[0]
    chunks = ([(kc_ref, vtc_ref, o) for o in range(0, kc_ref.shape[0], TKC)]
              + [(k_ref, vt_ref, o) for o in range(0, k_ref.shape[0], TKC)])
    n_chunks = len(chunks)

    d = dl_ref[...]
    lam = (jnp.exp(jnp.sum(d[0:1] * d[1:2], axis=1, keepdims=True))
           - jnp.exp(jnp.sum(d[2:3] * d[3:4], axis=1, keepdims=True)) + LAM_INIT)

    def q_sub(qs, carry):
        r0 = pl.multiple_of(qs * QSUB, QSUB)
        qt = q_ref[pl.ds(r0, QSUB), :].astype(F32).T
        dim = lax.broadcasted_iota(jnp.int32, qt.shape, 0)
        qqt = jnp.concatenate([jnp.where(dim < HEAD_DIM, qt, 0.0),
                               jnp.where(dim >= HEAD_DIM, qt, 0.0)], axis=1).astype(BF16)

        def scores(c):
            k_src, _, o = chunks[c]
            return jnp.dot(k_src[o:o + TKC, :], qqt, preferred_element_type=F32)

        def for_each_chunk(step):
            pending = {}
            issued = 0
            for c in range(n_chunks):
                while issued < min(c + LOOKAHEAD + 1, n_chunks):
                    pending[issued] = scores(issued)
                    issued += 1
                step(c, pending.pop(c))

        def pv_product(c, st, shift):
            _, v_src, o = chunks[c]
            p = jnp.exp2(st - shift)
            part = jnp.dot(v_src[:, o:o + TKC], p.astype(BF16), preferred_element_type=F32)
            return part, jnp.sum(p.reshape(TKC // SUBLANES, SUBLANES, p.shape[1]), axis=0)

        def fixed_shift():
            state = {}

            def step(c, st):
                if c == 0:
                    state["m"] = jnp.max(st, axis=0, keepdims=True)
                    state["acc"], state["l"] = pv_product(c, st, state["m"])
                else:
                    part, lpart = pv_product(c, st, state["m"])
                    state["acc"] = state["acc"] + part
                    state["l"] = state["l"] + lpart

            for_each_chunk(step)
            return state["acc"], state["l"]

        def running_max():
            state = {"m": jnp.full((1, 2 * QSUB), NEG_BIG, F32),
                     "acc": jnp.zeros((DV, 2 * QSUB), F32),
                     "l": jnp.zeros((SUBLANES, 2 * QSUB), F32)}

            def step(c, st):
                m_new = jnp.maximum(state["m"], jnp.max(st, axis=0, keepdims=True))
                alpha = jnp.exp2(state["m"] - m_new)
                part, lpart = pv_product(c, st, m_new)
                state["acc"] = state["acc"] * alpha + part
                state["l"] = state["l"] * alpha + lpart
                state["m"] = m_new

            for_each_chunk(step)
            return state["acc"], state["l"]

        acc, l8 = fixed_shift()
        finite = lambda t: jnp.max(jnp.where(jnp.abs(t) < OVERFLOW_GUARD, 0.0, 1.0)) < 0.5
        acc, l8 = lax.cond(finite(acc) & finite(l8), lambda: (acc, l8), running_max)

        r = acc / jnp.sum(l8, axis=0, keepdims=True)
        ot = r[:, :QSUB] - lam * r[:, QSUB:]
        ot = ot * lax.rsqrt(jnp.mean(ot * ot, axis=0, keepdims=True) + SUBLN_EPS)
        o_ref[pl.ds(r0, QSUB), :] = (ot.T * sg_ref[...] * (1.0 - LAM_INIT)).astype(o_ref.dtype)
        return carry

    lax.fori_loop(0, tq // QSUB, q_sub, 0)


def _attn(q, kc, k, vtc, vt, diff_lambda, subln_g, tq):
    s = q.shape[0]
    n_ctx = kc.shape[0]
    assert n_ctx % TKC == 0 and s % TKC == 0 and tq % QSUB == 0
    head_rows = lambda n: pl.BlockSpec((n, DV), lambda h, i: (0, h))
    head_cols = lambda n: pl.BlockSpec((DV, n), lambda h, i: (h, 0))
    return pl.pallas_call(
        _attn_kernel,
        grid=(N_HEADS, s // tq),
        in_specs=[pl.BlockSpec((tq, DV), lambda h, i: (i, h)),
                  head_rows(n_ctx), head_rows(s), head_cols(n_ctx), head_cols(s),
                  pl.BlockSpec((4, HEAD_DIM), lambda h, i: (0, 0)),
                  pl.BlockSpec((1, DV), lambda h, i: (0, 0))],
        out_specs=pl.BlockSpec((tq, DV), lambda h, i: (i, h)),
        out_shape=jax.ShapeDtypeStruct((s, N_HEADS * DV), BF16),
        compiler_params=_params(("arbitrary", "arbitrary")),
        name="attn",
    )(q, kc, k, vtc, vt, diff_lambda, subln_g)


def _log1p(u):
    w = 1.0 + u
    return jnp.where(w == 1.0, u, jnp.log(w) * (u / (w - 1.0)))


def _softplus(z):
    return jnp.maximum(z, 0.0) + _log1p(jnp.exp(-jnp.abs(z)))


def _neg_expm1(u):
    e = jnp.exp(u)
    d = e - 1.0
    safe = jnp.where(d == 0.0, 1.0, jnp.log(e))
    return -jnp.where(d == 0.0, u, d * (u / safe))


def _rglru_kernel(cur_f, prev_f, next_f, cur_b, prev_b, next_b, cw_ref, cb_ref, w_ref,
                  ba_ref, bi_ref, lam_ref, h0_ref, hf_ref, hb_ref, hl_ref,
                  e_scr, a_f, b_f, a_b, b_b, h_scr, *, halo):
    i = pl.program_id(0)
    n = pl.num_programs(0)
    t_rows = cur_f.shape[0]

    @pl.when(i == 0)
    def _():
        h_scr[...] = h0_ref[...]

    def conv(cur, prev, nxt, first, last):
        e_scr[0:halo, :] = jnp.where(first, 0.0, prev[...])
        e_scr[halo:halo + t_rows, :] = cur[...]
        e_scr[halo + t_rows:, :] = jnp.where(last, 0.0, nxt[...])
        y = cb_ref[...] + cw_ref[0:1, :] * e_scr[halo - 2:halo - 2 + t_rows, :]
        y = y + cw_ref[1:2, :] * e_scr[halo - 1:halo - 1 + t_rows, :]
        y = y + cw_ref[2:3, :] * e_scr[halo:halo + t_rows, :]
        y = y + cw_ref[3:4, :] * e_scr[halo + 1:halo + 1 + t_rows, :]
        return y

    def gates(xr, d, a_scr, b_scr):
        xb = xr.astype(BF16)
        sp = _softplus(-lam_ref[d:d + 1, :])
        for k in range(N_REC_BLOCKS):
            sl = slice(k * REC_BLOCK, (k + 1) * REC_BLOCK)
            pre = jnp.dot(xb[:, sl], w_ref[d, k], preferred_element_type=F32)
            r = jax.nn.sigmoid(pre[:, :REC_BLOCK] + ba_ref[d:d + 1, sl])
            ig = jax.nn.sigmoid(pre[:, REC_BLOCK:] + bi_ref[d:d + 1, sl])
            log_a = (-RG_C) * r * sp[:, sl]
            a_scr[:, sl] = jnp.exp(log_a)
            b_scr[:, sl] = jnp.sqrt(_neg_expm1(2.0 * log_a)) * (ig * xr[:, sl])

    gates(conv(cur_f, prev_f, next_f, i == 0, i == n - 1), 0, a_f, b_f)
    gates(conv(cur_b, prev_b, next_b, i == n - 1, i == 0), 1, a_b, b_b)

    def step(t, carry):
        hf, hb = carry
        hf = a_f[pl.ds(t, 1), :] * hf + b_f[pl.ds(t, 1), :]
        hf_ref[pl.ds(t, 1), :] = hf
        tb = t_rows - 1 - t
        hb = a_b[pl.ds(tb, 1), :] * hb + b_b[pl.ds(tb, 1), :]
        hb_ref[pl.ds(tb, 1), :] = hb
        return hf, hb

    hf, hb = lax.fori_loop(0, t_rows, step, (h_scr[0:1, :], h_scr[1:2, :]), unroll=8)
    h_scr[0:1, :] = hf
    h_scr[1:2, :] = hb
    hl_ref[...] = h_scr[...]


def _rglru(rx, cw, cb, wcat, ba, bi, lam, h0, t_rows):
    s, c = rx.shape
    n = s // t_rows
    halo = 8
    bpc = t_rows // halo
    nhb = s // halo
    cur = lambda f: pl.BlockSpec((t_rows, c), lambda i: (f(i, n), 0))
    prv = lambda f: pl.BlockSpec((halo, c), lambda i: (jnp.maximum(f(i, n) * bpc - 1, 0), 0))
    nxt = lambda f: pl.BlockSpec((halo, c), lambda i: (jnp.minimum((f(i, n) + 1) * bpc, nhb - 1), 0))
    fwd = lambda i, n: i
    bwd = lambda i, n: n - 1 - i
    full = lambda shape: pl.BlockSpec(shape, lambda i: (0,) * len(shape))
    return pl.pallas_call(
        functools.partial(_rglru_kernel, halo=halo),
        grid=(n,),
        in_specs=[cur(fwd), prv(fwd), nxt(fwd), cur(bwd), prv(bwd), nxt(bwd),
                  full(cw.shape), full((1, c)), full(wcat.shape),
                  full((2, c)), full((2, c)), full((2, c)), full((8, c))],
        out_specs=[pl.BlockSpec((t_rows, c), lambda i: (i, 0)),
                   pl.BlockSpec((t_rows, c), lambda i: (n - 1 - i, 0)),
                   full((8, c))],
        out_shape=[jax.ShapeDtypeStruct((s, c), F32), jax.ShapeDtypeStruct((s, c), F32),
                   jax.ShapeDtypeStruct((8, c), F32)],
        scratch_shapes=[pltpu.VMEM((t_rows + 2 * halo, c), F32)]
                       + [pltpu.VMEM((t_rows, c), F32)] * 4
                       + [pltpu.VMEM((8, c), F32)],
        compiler_params=_params(("arbitrary",)),
        name="rglru",
    )(rx, rx, rx, rx, rx, rx, cw, cb.reshape(1, c), wcat, ba, bi, lam, h0)


def _outproj_kernel(attn_ref, hf_ref, hb_ref, rz_ref, x_ref, w_ref, g1_ref, n2_ref,
                    sh_ref, sc_ref, x1_ref, h2_ref):
    half = attn_ref.shape[1]
    rec = ((hf_ref[...] + hb_ref[...]) * _gelu_tanh(rz_ref[...])).astype(BF16)
    y = jnp.dot(attn_ref[...], w_ref[:half, :], preferred_element_type=F32)
    y = y + jnp.dot(rec, w_ref[half:, :], preferred_element_type=F32)
    x1 = x_ref[...] + g1_ref[...] * y
    x1_ref[...] = x1
    h2 = _rms(x1, EPS) * n2_ref[...]
    h2_ref[...] = (h2 * (1.0 + sc_ref[...]) + sh_ref[...]).astype(BF16)


def _outproj(attn, hf, hb, rz, x2, w_out, g1, n2, sh2, sc2, tm):
    s, d = x2.shape
    half = attn.shape[1]
    row = lambda i: (i, 0)
    vec = pl.BlockSpec((1, d), lambda i: (0, 0))
    return pl.pallas_call(
        _outproj_kernel,
        grid=(s // tm,),
        in_specs=[pl.BlockSpec((tm, half), row)] * 4
                 + [pl.BlockSpec((tm, d), row), pl.BlockSpec(w_out.shape, lambda i: (0, 0)),
                    vec, vec, vec, vec],
        out_specs=[pl.BlockSpec((tm, d), row), pl.BlockSpec((tm, d), row)],
        out_shape=[jax.ShapeDtypeStruct((s, d), F32), jax.ShapeDtypeStruct((s, d), BF16)],
        compiler_params=_params(("arbitrary",)),
        name="outproj",
    )(attn, hf, hb, rz, x2, w_out, g1, n2, sh2, sc2)


def _ffn_kernel(h_ref, hp_ref, hn_ref, x1_ref, wu_ref, wg_ref, wd_ref, cw_ref, cb_ref,
                g2_ref, fg_ref, o_ref, he, acc, act, u_scr, g_scr, *, halo, nf):
    i = pl.program_id(0)
    f = pl.program_id(1)
    tm = h_ref.shape[0]

    def up_gate():
        u_scr[...] = jnp.dot(h_ref[...], wu_ref[...], preferred_element_type=F32)
        g_scr[...] = jnp.dot(he[...], wg_ref[...], preferred_element_type=F32)

    def geglu_into(slot):
        g = g_scr[...]
        rows = g.shape[0]
        gm = pltpu.roll(g, 1, 0)
        gp = pltpu.roll(g, rows - 1, 0)
        conv = cb_ref[...] + cw_ref[0:1, :] * gm + cw_ref[1:2, :] * g + cw_ref[2:3, :] * gp
        act[slot] = (_gelu_tanh(conv[halo:halo + tm, :]) * u_scr[...]).astype(BF16)

    def down_from(slot):
        acc[...] += jnp.dot(act[slot], wd_ref[...], preferred_element_type=F32)

    @pl.when(f == 0)
    def _():
        zero = jnp.zeros(hp_ref.shape, BF16)
        he[0:halo, :] = jnp.where(i == 0, zero, hp_ref[...])
        he[halo:halo + tm, :] = h_ref[...]
        he[halo + tm:, :] = jnp.where(i == pl.num_programs(0) - 1, zero, hn_ref[...])
        acc[...] = jnp.zeros(acc.shape, F32)
        up_gate()
        geglu_into(0)

    for slot in (0, 1):
        @pl.when((f > 0) & (f < nf) & (f % 2 == slot))
        def _():
            up_gate()
            down_from(1 - slot)
            geglu_into(slot)

    @pl.when(f == nf)
    def _():
        down_from((nf - 1) % 2)
        x2 = x1_ref[...] + g2_ref[...] * acc[...]
        o_ref[...] = _rms(x2, EPS) * fg_ref[...]


def _ffn(h2, x1, w_up, w_gate, w_down, cw, cb, g2, fg, tm, tf):
    s, d = h2.shape
    nf = w_up.shape[1] // tf
    halo = 16
    bpc = tm // halo
    nhb = s // halo
    row = lambda i, f: (i, 0)
    vec = pl.BlockSpec((1, d), lambda i, f: (0, 0))
    up_blk = lambda i, f: (0, jnp.minimum(f, nf - 1))
    return pl.pallas_call(
        functools.partial(_ffn_kernel, halo=halo, nf=nf),
        grid=(s // tm, nf + 1),
        in_specs=[pl.BlockSpec((tm, d), row),
                  pl.BlockSpec((halo, d), lambda i, f: (jnp.maximum(i * bpc - 1, 0), 0)),
                  pl.BlockSpec((halo, d), lambda i, f: (jnp.minimum((i + 1) * bpc, nhb - 1), 0)),
                  pl.BlockSpec((tm, d), row),
                  pl.BlockSpec((d, tf), up_blk),
                  pl.BlockSpec((d, tf), up_blk),
                  pl.BlockSpec((tf, d), lambda i, f: (jnp.maximum(f - 1, 0), 0)),
                  pl.BlockSpec((3, tf), up_blk),
                  pl.BlockSpec((1, tf), up_blk),
                  vec, vec],
        out_specs=pl.BlockSpec((tm, d), row),
        out_shape=jax.ShapeDtypeStruct((s, d), F32),
        scratch_shapes=[pltpu.VMEM((tm + 2 * halo, d), BF16), pltpu.VMEM((tm, d), F32),
                        pltpu.VMEM((2, tm, tf), BF16), pltpu.VMEM((tm, tf), F32),
                        pltpu.VMEM((tm + 2 * halo, tf), F32)],
        compiler_params=_params(("arbitrary", "arbitrary")),
        name="ffn",
    )(h2, h2, h2, x1, w_up, w_gate, w_down, cw, cb, g2, fg)


def _rope_tables(s):
    pos = jnp.arange(s)[:, None]
    lane = jnp.arange(DV)[None, :]
    inv = ROPE_THETA ** (-(lane % N_FREQ).astype(F32) / N_FREQ)
    is_col = (lane % HEAD_DIM) >= HEAD_DIM // 2
    ang = jnp.where(is_col, pos % GRID_W, pos // GRID_W).astype(F32) * inv
    sign = jnp.where((lane % (2 * N_FREQ)) < N_FREQ, -1.0, 1.0)
    return jnp.cos(ang), jnp.sin(ang) * sign


def kernel(x, c, ctx, c_ctx, w_ada, b_ada, norm1_g, w_in, rec_conv_w, rec_conv_b, rg_wa, rg_ba,
           rg_wi, rg_bi, rg_lambda, diff_lambda, subln_g, w_out, norm2_g, w_up, w_gate,
           ffn_conv_w, ffn_conv_b, w_down, final_g):
    assert x.shape[0] == 1 and w_ada.shape[0] == 1, "single batch element, single layer"
    s, d = x.shape[1:]
    n_ctx = ctx.shape[1]
    x2 = x[0]
    cx2 = ctx[0]

    cc = jnp.zeros((8, d), F32).at[0].set(c[0]).at[1].set(c_ctx)
    mod = _ada(cc, w_ada[0], b_ada[0])
    sh1, sc1, g1, sh2, sc2, g2 = [mod[0:1, j * d:(j + 1) * d] for j in range(6)]
    csh1, csc1 = mod[1:2, 0:d], mod[1:2, d:2 * d]

    w_in_b = w_in[0].astype(BF16)
    wcol = w_in_b.shape[1] // 5
    wvt = w_in[0, :, 2 * wcol:3 * wcol].T.astype(BF16)
    n1 = norm1_g[0].reshape(1, d)
    cos, sin = _rope_tables(s)
    q, k, vt, rx, rz = _inproj(x2, n1, sh1, sc1, w_in_b, wvt, cos, sin, _tile(s, 512))
    _, kc, vtc, rxc, _ = _inproj(cx2, n1, csh1, csc1, w_in_b, wvt, jnp.ones((n_ctx, DV), F32),
                                 jnp.zeros((n_ctx, DV), F32), n_ctx)

    attn = _attn(q, kc, k, vtc, vt, diff_lambda[0], subln_g[0].reshape(1, DV), _tile(s, 1024))

    wcat = jnp.concatenate([rg_wa[0], rg_wi[0]], axis=-1).astype(BF16)
    rec_args = (rec_conv_w[0], rec_conv_b[0], wcat, rg_ba[0], rg_bi[0], rg_lambda[0])
    _, _, h_ctx = _rglru(rxc, *rec_args, jnp.zeros((8, rxc.shape[1]), F32), n_ctx)
    hf, hb, _ = _rglru(rx, *rec_args, h_ctx, _tile(s, 256))

    x1, h2 = _outproj(attn, hf, hb, rz, x2, w_out[0].astype(BF16), g1,
                      norm2_g[0].reshape(1, d), sh2, sc2, _tile(s, 512))

    dff = w_up.shape[2]
    tf = 512
    pad = (-dff) % tf
    w_up_b = jnp.pad(w_up[0], ((0, 0), (0, pad))).astype(BF16)
    w_gate_b = jnp.pad(w_gate[0], ((0, 0), (0, pad))).astype(BF16)
    w_down_b = jnp.pad(w_down[0], ((0, pad), (0, 0))).astype(BF16)
    cw = jnp.pad(ffn_conv_w[0], ((0, 0), (0, pad)))
    cb = jnp.pad(ffn_conv_b[0], ((0, pad),)).reshape(1, dff + pad)
    out = _ffn(h2, x1, w_up_b, w_gate_b, w_down_b, cw, cb, g2, final_g.reshape(1, d),
               _tile(s, 512), tf)
    return out[None]
```

```python
import functools
import math

import jax
import jax.numpy as jnp
from jax import lax
from jax.experimental import pallas as pl
from jax.experimental.pallas import tpu as pltpu

F32 = jnp.float32
BF16 = jnp.bfloat16

GRID_W = 64
N_HEADS = 8
DV = 128
HEAD_DIM = DV // 2
N_FREQ = HEAD_DIM // 4
N_REC_BLOCKS = 8
REC_BLOCK = 128
RG_C = 8.0
ROPE_THETA = 10000.0
EPS = 1e-6
SUBLN_EPS = 1e-5
LAM_INIT = 0.8 - 0.6 * math.exp(-0.3 * 0)
LOG2E = 1.4426950408889634
NEG_BIG = -1e30

LANES = 128
SUBLANES = 8
MXU_N = 256
VMEM_LIMIT = 56 * 1024 * 1024


def _params(sem, vmem=VMEM_LIMIT):
    return pltpu.CompilerParams(dimension_semantics=sem, vmem_limit_bytes=vmem)


def _tile(n, pref):
    if n <= pref:
        return n
    t = pref
    while t >= 16:
        if n % t == 0:
            return t
        t -= 16
    return n


def _gelu_tanh(x):
    k0 = -2.0 * math.sqrt(2.0 / math.pi) * LOG2E
    return x / (1.0 + jnp.exp2(x * (k0 + (k0 * 0.044715) * (x * x))))


def _rms(x, eps):
    return x * lax.rsqrt(jnp.mean(x * x, axis=-1, keepdims=True) + eps)


def _ada_kernel(c_ref, w_ref, b_ref, o_ref):
    cc = c_ref[...]
    s = cc * jax.nn.sigmoid(cc)
    o_ref[...] = jnp.dot(s, w_ref[...], preferred_element_type=F32,
                         precision=lax.Precision.HIGHEST) + b_ref[...]


def _ada(cc, w_ada, b_ada):
    d, n = w_ada.shape
    tn = _tile(n, 1024)
    return pl.pallas_call(
        _ada_kernel,
        grid=(n // tn,),
        in_specs=[pl.BlockSpec((8, d), lambda j: (0, 0)),
                  pl.BlockSpec((d, tn), lambda j: (0, j)),
                  pl.BlockSpec((1, tn), lambda j: (0, j))],
        out_specs=pl.BlockSpec((8, tn), lambda j: (0, j)),
        out_shape=jax.ShapeDtypeStruct((8, n), F32),
        compiler_params=_params(("arbitrary",)),
        name="ada",
    )(cc, w_ada, b_ada.reshape(1, n))


def _inproj_kernel(x_ref, g_ref, sh_ref, sc_ref, w_ref, wvt_ref, cos_ref, sin_ref,
                   q_ref, k_ref, vt_ref, rx_ref, rz_ref, h_scr, *, q_scale):
    j = pl.program_id(1)
    tm = x_ref.shape[0]

    @pl.when(j == 0)
    def _():
        y = _rms(x_ref[...], EPS) * g_ref[...]
        h_scr[...] = (y * (1.0 + sc_ref[...]) + sh_ref[...]).astype(BF16)

    def proj():
        return jnp.dot(h_scr[...], w_ref[...], preferred_element_type=F32)

    def rope_into(o_ref, scale):
        acc = proj()
        cos = cos_ref[...]
        sin = sin_ref[...]
        lane = lax.broadcasted_iota(jnp.int32, (tm, LANES), 1)
        first = (lane % 32) < 16
        for g in range(acc.shape[1] // LANES):
            t = acc[:, g * LANES:(g + 1) * LANES]
            partner = jnp.where(first, pltpu.roll(t, LANES - 16, 1), pltpu.roll(t, 16, 1))
            r = t * cos + partner * sin
            if scale != 1.0:
                r = r * scale
            o_ref[:, g * LANES:(g + 1) * LANES] = r.astype(o_ref.dtype)

    @pl.when(j == 0)
    def _():
        rope_into(q_ref, q_scale)

    @pl.when(j == 1)
    def _():
        rope_into(k_ref, 1.0)

    @pl.when(j == 2)
    def _():
        vt_ref[...] = lax.dot_general(wvt_ref[...], h_scr[...], (((1,), (1,)), ((), ())),
                                      preferred_element_type=F32).astype(vt_ref.dtype)

    @pl.when(j == 3)
    def _():
        rx_ref[...] = proj()

    @pl.when(j == 4)
    def _():
        rz_ref[...] = proj()


def _inproj(x2, g, sh, sc, w_in, wvt, cos, sin, tm):
    s, d = x2.shape
    wcol = w_in.shape[1] // 5
    row = lambda i, j: (i, 0)
    vec = pl.BlockSpec((1, d), lambda i, j: (0, 0))
    out = lambda dt: jax.ShapeDtypeStruct((s, wcol), dt)
    return pl.pallas_call(
        functools.partial(_inproj_kernel, q_scale=HEAD_DIM ** -0.5 * LOG2E),
        grid=(s // tm, 5),
        in_specs=[pl.BlockSpec((tm, d), row), vec, vec, vec,
                  pl.BlockSpec((d, wcol), lambda i, j: (0, jnp.where(j == 2, 1, j))),
                  pl.BlockSpec(wvt.shape, lambda i, j: (0, 0)),
                  pl.BlockSpec((tm, LANES), row), pl.BlockSpec((tm, LANES), row)],
        out_specs=[pl.BlockSpec((tm, wcol), row), pl.BlockSpec((tm, wcol), row),
                   pl.BlockSpec((wcol, tm), lambda i, j: (0, i)),
                   pl.BlockSpec((tm, wcol), row), pl.BlockSpec((tm, wcol), row)],
        out_shape=[out(BF16), out(BF16), jax.ShapeDtypeStruct((wcol, s), BF16), out(F32), out(F32)],
        scratch_shapes=[pltpu.VMEM((tm, d), BF16)],
        compiler_params=_params(("arbitrary", "arbitrary")),
        name="inproj",
    )(x2, g, sh, sc, w_in, wvt, cos, sin)


TKC = MXU_N
QSUB = MXU_N
LOOKAHEAD = 2
OVERFLOW_GUARD = 2.0 ** 100


def _attn_kernel(q_ref, kc_ref, k_ref, vtc_ref, vt_ref, dl_ref, sg_ref, o_ref):
    tq = q_ref.shape[0]
    chunks = ([(kc_ref, vtc_ref, o) for o in range(0, kc_ref.shape[0], TKC)]
              + [(k_ref, vt_ref, o) for o in range(0, k_ref.shape[0], TKC)])
    n_chunks = len(chunks)

    d = dl_ref[...]
    lam = (jnp.exp(jnp.sum(d[0:1] * d[1:2], axis=1, keepdims=True))
           - jnp.exp(jnp.sum(d[2:3] * d[3:4], axis=1, keepdims=True)) + LAM_INIT)

    def q_sub(qs, carry):
        r0 = pl.multiple_of(qs * QSUB, QSUB)
        qt = q_ref[pl.ds(r0, QSUB), :].astype(F32).T
        dim = lax.broadcasted_iota(jnp.int32, qt.shape, 0)
        qqt = jnp.concatenate([jnp.where(dim < HEAD_DIM, qt, 0.0),
                               jnp.where(dim >= HEAD_DIM, qt, 0.0)], axis=1).astype(BF16)

        def scores(c):
            k_src, _, o = chunks[c]
            return jnp.dot(k_src[o:o + TKC, :], qqt, preferred_element_type=F32)

        def for_each_chunk(step):
            pending = {}
            issued = 0
            for c in range(n_chunks):
                while issued < min(c + LOOKAHEAD + 1, n_chunks):
                    pending[issued] = scores(issued)
                    issued += 1
                step(c, pending.pop(c))

        def pv_product(c, st, shift):
            _, v_src, o = chunks[c]
            p = jnp.exp2(st - shift)
            part = jnp.dot(v_src[:, o:o + TKC], p.astype(BF16), preferred_element_type=F32)
            return part, jnp.sum(p.reshape(TKC // SUBLANES, SUBLANES, p.shape[1]), axis=0)

        def fixed_shift():
            state = {}

            def step(c, st):
                if c == 0:
                    state["m"] = jnp.max(st, axis=0, keepdims=True)
                    state["acc"], state["l"] = pv_product(c, st, state["m"])
                else:
                    part, lpart = pv_product(c, st, state["m"])
                    state["acc"] = state["acc"] + part
                    state["l"] = state["l"] + lpart

            for_each_chunk(step)
            return state["acc"], state["l"]

        def running_max():
            state = {"m": jnp.full((1, 2 * QSUB), NEG_BIG, F32),
                     "acc": jnp.zeros((DV, 2 * QSUB), F32),
                     "l": jnp.zeros((SUBLANES, 2 * QSUB), F32)}

            def step(c, st):
                m_new = jnp.maximum(state["m"], jnp.max(st, axis=0, keepdims=True))
                alpha = jnp.exp2(state["m"] - m_new)
                part, lpart = pv_product(c, st, m_new)
                state["acc"] = state["acc"] * alpha + part
                state["l"] = state["l"] * alpha + lpart
                state["m"] = m_new

            for_each_chunk(step)
            return state["acc"], state["l"]

        acc, l8 = fixed_shift()
        finite = lambda t: jnp.max(jnp.where(jnp.abs(t) < OVERFLOW_GUARD, 0.0, 1.0)) < 0.5
        acc, l8 = lax.cond(finite(acc) & finite(l8), lambda: (acc, l8), running_max)

        r = acc / jnp.sum(l8, axis=0, keepdims=True)
        ot = r[:, :QSUB] - lam * r[:, QSUB:]
        ot = ot * lax.rsqrt(jnp.mean(ot * ot, axis=0, keepdims=True) + SUBLN_EPS)
        o_ref[pl.ds(r0, QSUB), :] = (ot.T * sg_ref[...] * (1.0 - LAM_INIT)).astype(o_ref.dtype)
        return carry

    lax.fori_loop(0, tq // QSUB, q_sub, 0)


def _attn(q, kc, k, vtc, vt, diff_lambda, subln_g, tq):
    s = q.shape[0]
    n_ctx = kc.shape[0]
    assert n_ctx % TKC == 0 and s % TKC == 0 and tq % QSUB == 0
    head_rows = lambda n: pl.BlockSpec((n, DV), lambda h, i: (0, h))
    head_cols = lambda n: pl.BlockSpec((DV, n), lambda h, i: (h, 0))
    return pl.pallas_call(
        _attn_kernel,
        grid=(N_HEADS, s // tq),
        in_specs=[pl.BlockSpec((tq, DV), lambda h, i: (i, h)),
                  head_rows(n_ctx), head_rows(s), head_cols(n_ctx), head_cols(s),
                  pl.BlockSpec((4, HEAD_DIM), lambda h, i: (0, 0)),
                  pl.BlockSpec((1, DV), lambda h, i: (0, 0))],
        out_specs=pl.BlockSpec((tq, DV), lambda h, i: (i, h)),
        out_shape=jax.ShapeDtypeStruct((s, N_HEADS * DV), BF16),
        compiler_params=_params(("arbitrary", "arbitrary")),
        name="attn",
    )(q, kc, k, vtc, vt, diff_lambda, subln_g)


def _log1p(u):
    w = 1.0 + u
    return jnp.where(w == 1.0, u, jnp.log(w) * (u / (w - 1.0)))


def _softplus(z):
    return jnp.maximum(z, 0.0) + _log1p(jnp.exp(-jnp.abs(z)))


def _neg_expm1(u, e):
    d = e - 1.0
    safe = jnp.where(d == 0.0, 1.0, jnp.log(e))
    return -jnp.where(d == 0.0, u, d * (u / safe))


def _rglru_kernel(cur_f, prev_f, next_f, cur_b, prev_b, next_b, cw_ref, cb_ref, w_ref,
                  ba_ref, bi_ref, lam_ref, h0_ref, hf_ref, hb_ref, hl_ref,
                  e_scr, a_f, b_f, a_b, b_b, h_scr, *, halo):
    i = pl.program_id(0)
    n = pl.num_programs(0)
    t_rows = cur_f.shape[0]

    @pl.when(i == 0)
    def _():
        h_scr[...] = h0_ref[...]

    def conv(cur, prev, nxt, first, last):
        e_scr[0:halo, :] = jnp.where(first, 0.0, prev[...])
        e_scr[halo:halo + t_rows, :] = cur[...]
        e_scr[halo + t_rows:, :] = jnp.where(last, 0.0, nxt[...])
        y = cb_ref[...] + cw_ref[0:1, :] * e_scr[halo - 2:halo - 2 + t_rows, :]
        y = y + cw_ref[1:2, :] * e_scr[halo - 1:halo - 1 + t_rows, :]
        y = y + cw_ref[2:3, :] * e_scr[halo:halo + t_rows, :]
        y = y + cw_ref[3:4, :] * e_scr[halo + 1:halo + 1 + t_rows, :]
        return y

    def gates(xr, d, a_scr, b_scr):
        xb = xr.astype(BF16)
        sp = _softplus(-lam_ref[d:d + 1, :])
        for k in range(N_REC_BLOCKS):
            sl = slice(k * REC_BLOCK, (k + 1) * REC_BLOCK)
            pre = jnp.dot(xb[:, sl], w_ref[d, k], preferred_element_type=F32)
            r = jax.nn.sigmoid(pre[:, :REC_BLOCK] + ba_ref[d:d + 1, sl])
            ig = jax.nn.sigmoid(pre[:, REC_BLOCK:] + bi_ref[d:d + 1, sl])
            log_a = (-RG_C) * r * sp[:, sl]
            a = jnp.exp(log_a)
            a_scr[:, sl] = a
            b_scr[:, sl] = jnp.sqrt(_neg_expm1(2.0 * log_a, a * a)) * (ig * xr[:, sl])

    gates(conv(cur_f, prev_f, next_f, i == 0, i == n - 1), 0, a_f, b_f)
    gates(conv(cur_b, prev_b, next_b, i == n - 1, i == 0), 1, a_b, b_b)

    def step(t, carry):
        hf, hb = carry
        hf = a_f[pl.ds(t, 1), :] * hf + b_f[pl.ds(t, 1), :]
        hf_ref[pl.ds(t, 1), :] = hf
        tb = t_rows - 1 - t
        hb = a_b[pl.ds(tb, 1), :] * hb + b_b[pl.ds(tb, 1), :]
        hb_ref[pl.ds(tb, 1), :] = hb
        return hf, hb

    hf, hb = lax.fori_loop(0, t_rows, step, (h_scr[0:1, :], h_scr[1:2, :]), unroll=8)
    h_scr[0:1, :] = hf
    h_scr[1:2, :] = hb
    hl_ref[...] = h_scr[...]


def _rglru(rx, cw, cb, wcat, ba, bi, lam, h0, t_rows):
    s, c = rx.shape
    n = s // t_rows
    halo = 8
    bpc = t_rows // halo
    nhb = s // halo
    cur = lambda f: pl.BlockSpec((t_rows, c), lambda i: (f(i, n), 0))
    prv = lambda f: pl.BlockSpec((halo, c), lambda i: (jnp.maximum(f(i, n) * bpc - 1, 0), 0))
    nxt = lambda f: pl.BlockSpec((halo, c), lambda i: (jnp.minimum((f(i, n) + 1) * bpc, nhb - 1), 0))
    fwd = lambda i, n: i
    bwd = lambda i, n: n - 1 - i
    full = lambda shape: pl.BlockSpec(shape, lambda i: (0,) * len(shape))
    return pl.pallas_call(
        functools.partial(_rglru_kernel, halo=halo),
        grid=(n,),
        in_specs=[cur(fwd), prv(fwd), nxt(fwd), cur(bwd), prv(bwd), nxt(bwd),
                  full(cw.shape), full((1, c)), full(wcat.shape),
                  full((2, c)), full((2, c)), full((2, c)), full((8, c))],
        out_specs=[pl.BlockSpec((t_rows, c), lambda i: (i, 0)),
                   pl.BlockSpec((t_rows, c), lambda i: (n - 1 - i, 0)),
                   full((8, c))],
        out_shape=[jax.ShapeDtypeStruct((s, c), F32), jax.ShapeDtypeStruct((s, c), F32),
                   jax.ShapeDtypeStruct((8, c), F32)],
        scratch_shapes=[pltpu.VMEM((t_rows + 2 * halo, c), F32)]
                       + [pltpu.VMEM((t_rows, c), F32)] * 4
                       + [pltpu.VMEM((8, c), F32)],
        compiler_params=_params(("arbitrary",)),
        name="rglru",
    )(rx, rx, rx, rx, rx, rx, cw, cb.reshape(1, c), wcat, ba, bi, lam, h0)


def _outproj_kernel(attn_ref, hf_ref, hb_ref, rz_ref, x_ref, w_ref, g1_ref, n2_ref,
                    sh_ref, sc_ref, x1_ref, h2_ref):
    half = attn_ref.shape[1]
    rec = ((hf_ref[...] + hb_ref[...]) * _gelu_tanh(rz_ref[...])).astype(BF16)
    y = jnp.dot(attn_ref[...], w_ref[:half, :], preferred_element_type=F32)
    y = y + jnp.dot(rec, w_ref[half:, :], preferred_element_type=F32)
    x1 = x_ref[...] + g1_ref[...] * y
    x1_ref[...] = x1
    h2 = _rms(x1, EPS) * n2_ref[...]
    h2_ref[...] = (h2 * (1.0 + sc_ref[...]) + sh_ref[...]).astype(BF16)


def _outproj(attn, hf, hb, rz, x2, w_out, g1, n2, sh2, sc2, tm):
    s, d = x2.shape
    half = attn.shape[1]
    row = lambda i: (i, 0)
    vec = pl.BlockSpec((1, d), lambda i: (0, 0))
    return pl.pallas_call(
        _outproj_kernel,
        grid=(s // tm,),
        in_specs=[pl.BlockSpec((tm, half), row)] * 4
                 + [pl.BlockSpec((tm, d), row), pl.BlockSpec(w_out.shape, lambda i: (0, 0)),
                    vec, vec, vec, vec],
        out_specs=[pl.BlockSpec((tm, d), row), pl.BlockSpec((tm, d), row)],
        out_shape=[jax.ShapeDtypeStruct((s, d), F32), jax.ShapeDtypeStruct((s, d), BF16)],
        compiler_params=_params(("arbitrary",)),
        name="outproj",
    )(attn, hf, hb, rz, x2, w_out, g1, n2, sh2, sc2)


def _ffn_kernel(h_ref, hp_ref, hn_ref, x1_ref, wu_ref, wg_ref, wd_ref, cw_ref, cb_ref,
                g2_ref, fg_ref, o_ref, he, acc, *, halo):
    i = pl.program_id(0)
    f = pl.program_id(1)
    tm = h_ref.shape[0]

    @pl.when(f == 0)
    def _():
        zero = jnp.zeros(hp_ref.shape, BF16)
        he[0:halo, :] = jnp.where(i == 0, zero, hp_ref[...])
        he[halo:halo + tm, :] = h_ref[...]
        he[halo + tm:, :] = jnp.where(i == pl.num_programs(0) - 1, zero, hn_ref[...])
        acc[...] = jnp.zeros(acc.shape, F32)

    u = jnp.dot(h_ref[...], wu_ref[...], preferred_element_type=F32)
    g = jnp.dot(he[...], wg_ref[...], preferred_element_type=F32)
    rows = g.shape[0]
    gm = pltpu.roll(g, 1, 0)
    gp = pltpu.roll(g, rows - 1, 0)
    conv = cb_ref[...] + cw_ref[0:1, :] * gm + cw_ref[1:2, :] * g + cw_ref[2:3, :] * gp
    act = (_gelu_tanh(conv[halo:halo + tm, :]) * u).astype(BF16)
    acc[...] += jnp.dot(act, wd_ref[...], preferred_element_type=F32)

    @pl.when(f == pl.num_programs(1) - 1)
    def _():
        x2 = x1_ref[...] + g2_ref[...] * acc[...]
        o_ref[...] = _rms(x2, EPS) * fg_ref[...]


def _ffn(h2, x1, w_up, w_gate, w_down, cw, cb, g2, fg, tm, tf):
    s, d = h2.shape
    dff = w_up.shape[1]
    halo = 16
    bpc = tm // halo
    nhb = s // halo
    row = lambda i, f: (i, 0)
    vec = pl.BlockSpec((1, d), lambda i, f: (0, 0))
    return pl.pallas_call(
        functools.partial(_ffn_kernel, halo=halo),
        grid=(s // tm, dff // tf),
        in_specs=[pl.BlockSpec((tm, d), row),
                  pl.BlockSpec((halo, d), lambda i, f: (jnp.maximum(i * bpc - 1, 0), 0)),
                  pl.BlockSpec((halo, d), lambda i, f: (jnp.minimum((i + 1) * bpc, nhb - 1), 0)),
                  pl.BlockSpec((tm, d), row),
                  pl.BlockSpec((d, tf), lambda i, f: (0, f)),
                  pl.BlockSpec((d, tf), lambda i, f: (0, f)),
                  pl.BlockSpec((tf, d), lambda i, f: (f, 0)),
                  pl.BlockSpec((3, tf), lambda i, f: (0, f)),
                  pl.BlockSpec((1, tf), lambda i, f: (0, f)),
                  vec, vec],
        out_specs=pl.BlockSpec((tm, d), row),
        out_shape=jax.ShapeDtypeStruct((s, d), F32),
        scratch_shapes=[pltpu.VMEM((tm + 2 * halo, d), BF16), pltpu.VMEM((tm, d), F32)],
        compiler_params=_params(("arbitrary", "arbitrary")),
        name="ffn",
    )(h2, h2, h2, x1, w_up, w_gate, w_down, cw, cb, g2, fg)


def _rope_tables(s):
    n_rows = s // GRID_W
    inv = ROPE_THETA ** (-jnp.arange(N_FREQ, dtype=F32) / N_FREQ)
    ang_r = jnp.arange(n_rows, dtype=F32)[:, None] * inv
    ang_c = jnp.arange(GRID_W, dtype=F32)[:, None] * inv
    by_row = lambda t: jnp.broadcast_to(t[:, None, :], (n_rows, GRID_W, N_FREQ)).reshape(s, N_FREQ)
    by_col = lambda t: jnp.broadcast_to(t[None, :, :], (n_rows, GRID_W, N_FREQ)).reshape(s, N_FREQ)
    cr, sr = by_row(jnp.cos(ang_r)), by_row(jnp.sin(ang_r))
    cc, sc = by_col(jnp.cos(ang_c)), by_col(jnp.sin(ang_c))
    reps = DV // HEAD_DIM
    return (jnp.concatenate([cr, cr, cc, cc] * reps, axis=1),
            jnp.concatenate([-sr, sr, -sc, sc] * reps, axis=1))


def kernel(x, c, ctx, c_ctx, w_ada, b_ada, norm1_g, w_in, rec_conv_w, rec_conv_b, rg_wa, rg_ba,
           rg_wi, rg_bi, rg_lambda, diff_lambda, subln_g, w_out, norm2_g, w_up, w_gate,
           ffn_conv_w, ffn_conv_b, w_down, final_g):
    assert x.shape[0] == 1 and w_ada.shape[0] == 1, "single batch element, single layer"
    s, d = x.shape[1:]
    n_ctx = ctx.shape[1]
    x2 = x[0]
    cx2 = ctx[0]

    cc = jnp.zeros((8, d), F32).at[0].set(c[0]).at[1].set(c_ctx)
    mod = _ada(cc, w_ada[0], b_ada[0])
    sh1, sc1, g1, sh2, sc2, g2 = [mod[0:1, j * d:(j + 1) * d] for j in range(6)]
    csh1, csc1 = mod[1:2, 0:d], mod[1:2, d:2 * d]

    w_in_b = w_in[0].astype(BF16)
    wcol = w_in_b.shape[1] // 5
    wvt = lax.optimization_barrier(w_in[0, :, 2 * wcol:3 * wcol]).T.astype(BF16)
    n1 = norm1_g[0].reshape(1, d)
    cos, sin = _rope_tables(s)
    q, k, vt, rx, rz = _inproj(x2, n1, sh1, sc1, w_in_b, wvt, cos, sin, _tile(s, 512))
    _, kc, vtc, rxc, _ = _inproj(cx2, n1, csh1, csc1, w_in_b, wvt, jnp.ones((n_ctx, DV), F32),
                                 jnp.zeros((n_ctx, DV), F32), n_ctx)

    attn = _attn(q, kc, k, vtc, vt, diff_lambda[0], subln_g[0].reshape(1, DV), _tile(s, 1024))

    wcat = jnp.concatenate([rg_wa[0], rg_wi[0]], axis=-1).astype(BF16)
    rec_args = (rec_conv_w[0], rec_conv_b[0], wcat, rg_ba[0], rg_bi[0], rg_lambda[0])
    _, _, h_ctx = _rglru(rxc, *rec_args, jnp.zeros((8, rxc.shape[1]), F32), n_ctx)
    hf, hb, _ = _rglru(rx, *rec_args, h_ctx, _tile(s, 256))

    x1, h2 = _outproj(attn, hf, hb, rz, x2, w_out[0].astype(BF16), g1,
                      norm2_g[0].reshape(1, d), sh2, sc2, _tile(s, 512))

    dff = w_up.shape[2]
    tf = 512
    pad = (-dff) % tf
    w_up_b = jnp.pad(w_up[0], ((0, 0), (0, pad))).astype(BF16)
    w_gate_b = jnp.pad(w_gate[0], ((0, 0), (0, pad))).astype(BF16)
    w_down_b = jnp.pad(w_down[0], ((0, pad), (0, 0))).astype(BF16)
    cw = jnp.pad(ffn_conv_w[0], ((0, 0), (0, pad)))
    cb = jnp.pad(ffn_conv_b[0], ((0, pad),)).reshape(1, dff + pad)
    out = _ffn(h2, x1, w_up_b, w_gate_b, w_down_b, cw, cb, g2, final_g.reshape(1, d),
               _tile(s, 512), tf)
    return out[None]
```

```python
import functools
import math

import jax
import jax.numpy as jnp
from jax import lax
from jax.experimental import pallas as pl
from jax.experimental.pallas import tpu as pltpu

F32 = jnp.float32
BF16 = jnp.bfloat16

GRID_W = 64
N_HEADS = 8
DV = 128
HEAD_DIM = DV // 2
N_FREQ = HEAD_DIM // 4
N_REC_BLOCKS = 8
REC_BLOCK = 128
RG_C = 8.0
ROPE_THETA = 10000.0
EPS = 1e-6
SUBLN_EPS = 1e-5
LAM_INIT = 0.8 - 0.6 * math.exp(-0.3 * 0)
LOG2E = 1.4426950408889634
NEG_BIG = -1e30

LANES = 128
SUBLANES = 8
MXU_N = 256
VMEM_LIMIT = 56 * 1024 * 1024


def _params(sem, vmem=VMEM_LIMIT):
    return pltpu.CompilerParams(dimension_semantics=sem, vmem_limit_bytes=vmem)


def _tile(n, pref):
    if n <= pref:
        return n
    t = pref
    while t >= 16:
        if n % t == 0:
            return t
        t -= 16
    return n


def _gelu_tanh(x):
    k0 = -2.0 * math.sqrt(2.0 / math.pi) * LOG2E
    return x / (1.0 + jnp.exp2(x * (k0 + (k0 * 0.044715) * (x * x))))


def _rms(x, eps):
    return x * lax.rsqrt(jnp.mean(x * x, axis=-1, keepdims=True) + eps)


def _ada_kernel(c_ref, w_ref, b_ref, o_ref):
    cc = c_ref[...]
    s = cc * jax.nn.sigmoid(cc)
    o_ref[...] = jnp.dot(s, w_ref[...], preferred_element_type=F32,
                         precision=lax.Precision.HIGHEST) + b_ref[...]


def _ada(cc, w_ada, b_ada):
    d, n = w_ada.shape
    tn = _tile(n, 1024)
    return pl.pallas_call(
        _ada_kernel,
        grid=(n // tn,),
        in_specs=[pl.BlockSpec((8, d), lambda j: (0, 0)),
                  pl.BlockSpec((d, tn), lambda j: (0, j)),
                  pl.BlockSpec((1, tn), lambda j: (0, j))],
        out_specs=pl.BlockSpec((8, tn), lambda j: (0, j)),
        out_shape=jax.ShapeDtypeStruct((8, n), F32),
        compiler_params=_params(("arbitrary",)),
        name="ada",
    )(cc, w_ada, b_ada.reshape(1, n))


def _inproj_kernel(x_ref, g_ref, sh_ref, sc_ref, w_ref, wvt_ref, cos_ref, sin_ref,
                   q_ref, k_ref, vt_ref, rx_ref, rz_ref, h_scr, *, q_scale):
    j = pl.program_id(1)
    tm = x_ref.shape[0]

    @pl.when(j == 0)
    def _():
        y = _rms(x_ref[...], EPS) * g_ref[...]
        h_scr[...] = (y * (1.0 + sc_ref[...]) + sh_ref[...]).astype(BF16)

    def proj():
        return jnp.dot(h_scr[...], w_ref[...], preferred_element_type=F32)

    def rope_into(o_ref, scale):
        acc = proj()
        cos = cos_ref[...]
        sin = sin_ref[...]
        lane = lax.broadcasted_iota(jnp.int32, (tm, LANES), 1)
        first = (lane % 32) < 16
        for g in range(acc.shape[1] // LANES):
            t = acc[:, g * LANES:(g + 1) * LANES]
            partner = jnp.where(first, pltpu.roll(t, LANES - 16, 1), pltpu.roll(t, 16, 1))
            r = t * cos + partner * sin
            if scale != 1.0:
                r = r * scale
            o_ref[:, g * LANES:(g + 1) * LANES] = r.astype(o_ref.dtype)

    @pl.when(j == 0)
    def _():
        rope_into(q_ref, q_scale)

    @pl.when(j == 1)
    def _():
        rope_into(k_ref, 1.0)

    @pl.when(j == 2)
    def _():
        vt_ref[...] = lax.dot_general(wvt_ref[...], h_scr[...], (((1,), (1,)), ((), ())),
                                      preferred_element_type=F32).astype(vt_ref.dtype)

    @pl.when(j == 3)
    def _():
        rx_ref[...] = proj()

    @pl.when(j == 4)
    def _():
        rz_ref[...] = proj()


def _inproj(x2, g, sh, sc, w_in, wvt, cos, sin, tm):
    s, d = x2.shape
    wcol = w_in.shape[1] // 5
    row = lambda i, j: (i, 0)
    vec = pl.BlockSpec((1, d), lambda i, j: (0, 0))
    out = lambda dt: jax.ShapeDtypeStruct((s, wcol), dt)
    return pl.pallas_call(
        functools.partial(_inproj_kernel, q_scale=HEAD_DIM ** -0.5 * LOG2E),
        grid=(s // tm, 5),
        in_specs=[pl.BlockSpec((tm, d), row), vec, vec, vec,
                  pl.BlockSpec((d, wcol), lambda i, j: (0, jnp.where(j == 2, 1, j))),
                  pl.BlockSpec(wvt.shape, lambda i, j: (0, 0)),
                  pl.BlockSpec((tm, LANES), row), pl.BlockSpec((tm, LANES), row)],
        out_specs=[pl.BlockSpec((tm, wcol), row), pl.BlockSpec((tm, wcol), row),
                   pl.BlockSpec((wcol, tm), lambda i, j: (0, i)),
                   pl.BlockSpec((tm, wcol), row), pl.BlockSpec((tm, wcol), row)],
        out_shape=[out(BF16), out(BF16), jax.ShapeDtypeStruct((wcol, s), BF16), out(F32), out(F32)],
        scratch_shapes=[pltpu.VMEM((tm, d), BF16)],
        compiler_params=_params(("arbitrary", "arbitrary")),
        name="inproj",
    )(x2, g, sh, sc, w_in, wvt, cos, sin)


TKC = MXU_N
QSUB = MXU_N
LOOKAHEAD = 2
OVERFLOW_GUARD = 2.0 ** 100


def _attn_kernel(q_ref, kc_ref, k_ref, vtc_ref, vt_ref, dl_ref, sg_ref, o_ref):
    tq = q_ref.shape[0]
    chunks = ([(kc_ref, vtc_ref, o) for o in range(0, kc_ref.shape[0], TKC)]
              + [(k_ref, vt_ref, o) for o in range(0, k_ref.shape[0], TKC)])
    n_chunks = len(chunks)

    d = dl_ref[...]
    lam = (jnp.exp(jnp.sum(d[0:1] * d[1:2], axis=1, keepdims=True))
           - jnp.exp(jnp.sum(d[2:3] * d[3:4], axis=1, keepdims=True)) + LAM_INIT)

    def q_sub(qs, carry):
        r0 = pl.multiple_of(qs * QSUB, QSUB)
        qt = q_ref[pl.ds(r0, QSUB), :].astype(F32).T
        dim = lax.broadcasted_iota(jnp.int32, qt.shape, 0)
        qqt = jnp.concatenate([jnp.where(dim < HEAD_DIM, qt, 0.0),
                               jnp.where(dim >= HEAD_DIM, qt, 0.0)], axis=1).astype(BF16)

        def scores(c):
            k_src, _, o = chunks[c]
            return jnp.dot(k_src[o:o + TKC, :], qqt, preferred_element_type=F32)

        def for_each_chunk(step):
            pending = {}
            issued = 0
            for c in range(n_chunks):
                while issued < min(c + LOOKAHEAD + 1, n_chunks):
                    pending[issued] = scores(issued)
                    issued += 1
                step(c, pending.pop(c))

        def pv_product(c, st, shift):
            _, v_src, o = chunks[c]
            p = jnp.exp2(st - shift)
            part = jnp.dot(v_src[:, o:o + TKC], p.astype(BF16), preferred_element_type=F32)
            return part, jnp.sum(p.reshape(TKC // SUBLANES, SUBLANES, p.shape[1]), axis=0)

        def fixed_shift():
            state = {}

            def step(c, st):
                if c == 0:
                    state["m"] = jnp.max(st, axis=0, keepdims=True)
                    state["acc"], state["l"] = pv_product(c, st, state["m"])
                else:
                    part, lpart = pv_product(c, st, state["m"])
                    state["acc"] = state["acc"] + part
                    state["l"] = state["l"] + lpart

            for_each_chunk(step)
            return state["acc"], state["l"]

        def running_max():
            state = {"m": jnp.full((1, 2 * QSUB), NEG_BIG, F32),
                     "acc": jnp.zeros((DV, 2 * QSUB), F32),
                     "l": jnp.zeros((SUBLANES, 2 * QSUB), F32)}

            def step(c, st):
                m_new = jnp.maximum(state["m"], jnp.max(st, axis=0, keepdims=True))
                alpha = jnp.exp2(state["m"] - m_new)
                part, lpart = pv_product(c, st, m_new)
                state["acc"] = state["acc"] * alpha + part
                state["l"] = state["l"] * alpha + lpart
                state["m"] = m_new

            for_each_chunk(step)
            return state["acc"], state["l"]

        acc, l8 = fixed_shift()
        finite = lambda t: jnp.max(jnp.where(jnp.abs(t) < OVERFLOW_GUARD, 0.0, 1.0)) < 0.5
        acc, l8 = lax.cond(finite(acc) & finite(l8), lambda: (acc, l8), running_max)

        r = acc / jnp.sum(l8, axis=0, keepdims=True)
        ot = r[:, :QSUB] - lam * r[:, QSUB:]
        ot = ot * lax.rsqrt(jnp.mean(ot * ot, axis=0, keepdims=True) + SUBLN_EPS)
        o_ref[pl.ds(r0, QSUB), :] = (ot.T * sg_ref[...] * (1.0 - LAM_INIT)).astype(o_ref.dtype)
        return carry

    lax.fori_loop(0, tq // QSUB, q_sub, 0)


def _attn(q, kc, k, vtc, vt, diff_lambda, subln_g, tq):
    s = q.shape[0]
    n_ctx = kc.shape[0]
    assert n_ctx % TKC == 0 and s % TKC == 0 and tq % QSUB == 0
    head_rows = lambda n: pl.BlockSpec((n, DV), lambda h, i: (0, h))
    head_cols = lambda n: pl.BlockSpec((DV, n), lambda h, i: (h, 0))
    return pl.pallas_call(
        _attn_kernel,
        grid=(N_HEADS, s // tq),
        in_specs=[pl.BlockSpec((tq, DV), lambda h, i: (i, h)),
                  head_rows(n_ctx), head_rows(s), head_cols(n_ctx), head_cols(s),
                  pl.BlockSpec((4, HEAD_DIM), lambda h, i: (0, 0)),
                  pl.BlockSpec((1, DV), lambda h, i: (0, 0))],
        out_specs=pl.BlockSpec((tq, DV), lambda h, i: (i, h)),
        out_shape=jax.ShapeDtypeStruct((s, N_HEADS * DV), BF16),
        compiler_params=_params(("arbitrary", "arbitrary")),
        name="attn",
    )(q, kc, k, vtc, vt, diff_lambda, subln_g)


def _log1p(u):
    w = 1.0 + u
    return jnp.where(w == 1.0, u, jnp.log(w) * (u / (w - 1.0)))


def _softplus(z):
    return jnp.maximum(z, 0.0) + _log1p(jnp.exp(-jnp.abs(z)))


def _neg_expm1(u, e):
    d = e - 1.0
    safe = jnp.where(d == 0.0, 1.0, jnp.log(e))
    return -jnp.where(d == 0.0, u, d * (u / safe))


def _rglru_kernel(cur_f, prev_f, next_f, cur_b, prev_b, next_b, cw_ref, cb_ref, w_ref,
                  ba_ref, bi_ref, lam_ref, h0_ref, hf_ref, hb_ref, hl_ref,
                  e_scr, a_f, b_f, a_b, b_b, h_scr, *, halo):
    i = pl.program_id(0)
    n = pl.num_programs(0)
    t_rows = cur_f.shape[0]

    @pl.when(i == 0)
    def _():
        h_scr[...] = h0_ref[...]

    def conv(cur, prev, nxt, first, last):
        e_scr[0:halo, :] = jnp.where(first, 0.0, prev[...])
        e_scr[halo:halo + t_rows, :] = cur[...]
        e_scr[halo + t_rows:, :] = jnp.where(last, 0.0, nxt[...])
        y = cb_ref[...] + cw_ref[0:1, :] * e_scr[halo - 2:halo - 2 + t_rows, :]
        y = y + cw_ref[1:2, :] * e_scr[halo - 1:halo - 1 + t_rows, :]
        y = y + cw_ref[2:3, :] * e_scr[halo:halo + t_rows, :]
        y = y + cw_ref[3:4, :] * e_scr[halo + 1:halo + 1 + t_rows, :]
        return y

    def gates(xr, d, a_scr, b_scr):
        xb = xr.astype(BF16)
        sp = _softplus(-lam_ref[d:d + 1, :])
        for k in range(N_REC_BLOCKS):
            sl = slice(k * REC_BLOCK, (k + 1) * REC_BLOCK)
            pre = jnp.dot(xb[:, sl], w_ref[d, k], preferred_element_type=F32)
            r = jax.nn.sigmoid(pre[:, :REC_BLOCK] + ba_ref[d:d + 1, sl])
            ig = jax.nn.sigmoid(pre[:, REC_BLOCK:] + bi_ref[d:d + 1, sl])
            log_a = (-RG_C) * r * sp[:, sl]
            a = jnp.exp(log_a)
            a_scr[:, sl] = a
            b_scr[:, sl] = jnp.sqrt(_neg_expm1(2.0 * log_a, a * a)) * (ig * xr[:, sl])

    gates(conv(cur_f, prev_f, next_f, i == 0, i == n - 1), 0, a_f, b_f)
    gates(conv(cur_b, prev_b, next_b, i == n - 1, i == 0), 1, a_b, b_b)

    def step(t, carry):
        hf, hb = carry
        hf = a_f[pl.ds(t, 1), :] * hf + b_f[pl.ds(t, 1), :]
        hf_ref[pl.ds(t, 1), :] = hf
        tb = t_rows - 1 - t
        hb = a_b[pl.ds(tb, 1), :] * hb + b_b[pl.ds(tb, 1), :]
        hb_ref[pl.ds(tb, 1), :] = hb
        return hf, hb

    hf, hb = lax.fori_loop(0, t_rows, step, (h_scr[0:1, :], h_scr[1:2, :]), unroll=8)
    h_scr[0:1, :] = hf
    h_scr[1:2, :] = hb
    hl_ref[...] = h_scr[...]


def _rglru(rx, cw, cb, wcat, ba, bi, lam, h0, t_rows):
    s, c = rx.shape
    n = s // t_rows
    halo = 8
    bpc = t_rows // halo
    nhb = s // halo
    cur = lambda f: pl.BlockSpec((t_rows, c), lambda i: (f(i, n), 0))
    prv = lambda f: pl.BlockSpec((halo, c), lambda i: (jnp.maximum(f(i, n) * bpc - 1, 0), 0))
    nxt = lambda f: pl.BlockSpec((halo, c), lambda i: (jnp.minimum((f(i, n) + 1) * bpc, nhb - 1), 0))
    fwd = lambda i, n: i
    bwd = lambda i, n: n - 1 - i
    full = lambda shape: pl.BlockSpec(shape, lambda i: (0,) * len(shape))
    return pl.pallas_call(
        functools.partial(_rglru_kernel, halo=halo),
        grid=(n,),
        in_specs=[cur(fwd), prv(fwd), nxt(fwd), cur(bwd), prv(bwd), nxt(bwd),
                  full(cw.shape), full((1, c)), full(wcat.shape),
                  full((2, c)), full((2, c)), full((2, c)), full((8, c))],
        out_specs=[pl.BlockSpec((t_rows, c), lambda i: (i, 0)),
                   pl.BlockSpec((t_rows, c), lambda i: (n - 1 - i, 0)),
                   full((8, c))],
        out_shape=[jax.ShapeDtypeStruct((s, c), F32), jax.ShapeDtypeStruct((s, c), F32),
                   jax.ShapeDtypeStruct((8, c), F32)],
        scratch_shapes=[pltpu.VMEM((t_rows + 2 * halo, c), F32)]
                       + [pltpu.VMEM((t_rows, c), F32)] * 4
                       + [pltpu.VMEM((8, c), F32)],
        compiler_params=_params(("arbitrary",)),
        name="rglru",
    )(rx, rx, rx, rx, rx, rx, cw, cb.reshape(1, c), wcat, ba, bi, lam, h0)


def _outproj_kernel(attn_ref, hf_ref, hb_ref, rz_ref, x_ref, w_ref, g1_ref, n2_ref,
                    sh_ref, sc_ref, x1_ref, h2_ref):
    half = attn_ref.shape[1]
    rec = ((hf_ref[...] + hb_ref[...]) * _gelu_tanh(rz_ref[...])).astype(BF16)
    y = jnp.dot(attn_ref[...], w_ref[:half, :], preferred_element_type=F32)
    y = y + jnp.dot(rec, w_ref[half:, :], preferred_element_type=F32)
    x1 = x_ref[...] + g1_ref[...] * y
    x1_ref[...] = x1
    h2 = _rms(x1, EPS) * n2_ref[...]
    h2_ref[...] = (h2 * (1.0 + sc_ref[...]) + sh_ref[...]).astype(BF16)


def _outproj(attn, hf, hb, rz, x2, w_out, g1, n2, sh2, sc2, tm):
    s, d = x2.shape
    half = attn.shape[1]
    row = lambda i: (i, 0)
    vec = pl.BlockSpec((1, d), lambda i: (0, 0))
    return pl.pallas_call(
        _outproj_kernel,
        grid=(s // tm,),
        in_specs=[pl.BlockSpec((tm, half), row)] * 4
                 + [pl.BlockSpec((tm, d), row), pl.BlockSpec(w_out.shape, lambda i: (0, 0)),
                    vec, vec, vec, vec],
        out_specs=[pl.BlockSpec((tm, d), row), pl.BlockSpec((tm, d), row)],
        out_shape=[jax.ShapeDtypeStruct((s, d), F32), jax.ShapeDtypeStruct((s, d), BF16)],
        compiler_params=_params(("arbitrary",)),
        name="outproj",
    )(attn, hf, hb, rz, x2, w_out, g1, n2, sh2, sc2)


def _ffn_kernel(h_ref, hp_ref, hn_ref, x1_ref, wu_ref, wg_ref, wd_ref, cw_ref, cb_ref,
                g2_ref, fg_ref, o_ref, he, acc, *, halo):
    i = pl.program_id(0)
    f = pl.program_id(1)
    tm = h_ref.shape[0]

    @pl.when(f == 0)
    def _():
        zero = jnp.zeros(hp_ref.shape, BF16)
        he[0:halo, :] = jnp.where(i == 0, zero, hp_ref[...])
        he[halo:halo + tm, :] = h_ref[...]
        he[halo + tm:, :] = jnp.where(i == pl.num_programs(0) - 1, zero, hn_ref[...])
        acc[...] = jnp.zeros(acc.shape, F32)

    u = jnp.dot(h_ref[...], wu_ref[...], preferred_element_type=F32)
    g = jnp.dot(he[...], wg_ref[...], preferred_element_type=F32)
    rows = g.shape[0]
    gm = pltpu.roll(g, 1, 0)
    gp = pltpu.roll(g, rows - 1, 0)
    conv = cb_ref[...] + cw_ref[0:1, :] * gm + cw_ref[1:2, :] * g + cw_ref[2:3, :] * gp
    act = (_gelu_tanh(conv[halo:halo + tm, :]) * u).astype(BF16)
    acc[...] += jnp.dot(act, wd_ref[...], preferred_element_type=F32)

    @pl.when(f == pl.num_programs(1) - 1)
    def _():
        x2 = x1_ref[...] + g2_ref[...] * acc[...]
        o_ref[...] = _rms(x2, EPS) * fg_ref[...]


def _ffn(h2, x1, w_up, w_gate, w_down, cw, cb, g2, fg, tm, tf):
    s, d = h2.shape
    dff = w_up.shape[1]
    halo = 16
    bpc = tm // halo
    nhb = s // halo
    row = lambda i, f: (i, 0)
    vec = pl.BlockSpec((1, d), lambda i, f: (0, 0))
    return pl.pallas_call(
        functools.partial(_ffn_kernel, halo=halo),
        grid=(s // tm, dff // tf),
        in_specs=[pl.BlockSpec((tm, d), row),
                  pl.BlockSpec((halo, d), lambda i, f: (jnp.maximum(i * bpc - 1, 0), 0)),
                  pl.BlockSpec((halo, d), lambda i, f: (jnp.minimum((i + 1) * bpc, nhb - 1), 0)),
                  pl.BlockSpec((tm, d), row),
                  pl.BlockSpec((d, tf), lambda i, f: (0, f)),
                  pl.BlockSpec((d, tf), lambda i, f: (0, f)),
                  pl.BlockSpec((tf, d), lambda i, f: (f, 0)),
                  pl.BlockSpec((3, tf), lambda i, f: (0, f)),
                  pl.BlockSpec((1, tf), lambda i, f: (0, f)),
                  vec, vec],
        out_specs=pl.BlockSpec((tm, d), row),
        out_shape=jax.ShapeDtypeStruct((s, d), F32),
        scratch_shapes=[pltpu.VMEM((tm + 2 * halo, d), BF16), pltpu.VMEM((tm, d), F32)],
        compiler_params=_params(("arbitrary", "arbitrary")),
        name="ffn",
    )(h2, h2, h2, x1, w_up, w_gate, w_down, cw, cb, g2, fg)


def _rope_tables(s):
    n_rows = s // GRID_W
    lane = jnp.arange(DV)
    inv = ROPE_THETA ** (-(lane % N_FREQ).astype(F32) / N_FREQ)
    is_col = (lane % HEAD_DIM) >= HEAD_DIM // 2
    sign = jnp.where((lane % (2 * N_FREQ)) < N_FREQ, -1.0, 1.0)
    ang_r = jnp.arange(n_rows, dtype=F32)[:, None] * inv
    ang_c = jnp.arange(GRID_W, dtype=F32)[:, None] * inv
    pick = lambda by_row, by_col: jnp.where(is_col, by_col[None, :, :], by_row[:, None, :]).reshape(s, DV)
    return (pick(jnp.cos(ang_r), jnp.cos(ang_c)),
            pick(jnp.sin(ang_r) * sign, jnp.sin(ang_c) * sign))


def kernel(x, c, ctx, c_ctx, w_ada, b_ada, norm1_g, w_in, rec_conv_w, rec_conv_b, rg_wa, rg_ba,
           rg_wi, rg_bi, rg_lambda, diff_lambda, subln_g, w_out, norm2_g, w_up, w_gate,
           ffn_conv_w, ffn_conv_b, w_down, final_g):
    assert x.shape[0] == 1 and w_ada.shape[0] == 1, "single batch element, single layer"
    s, d = x.shape[1:]
    n_ctx = ctx.shape[1]
    x2 = x[0]
    cx2 = ctx[0]

    cc = jnp.zeros((8, d), F32).at[0].set(c[0]).at[1].set(c_ctx)
    mod = _ada(cc, w_ada[0], b_ada[0])
    sh1, sc1, g1, sh2, sc2, g2 = [mod[0:1, j * d:(j + 1) * d] for j in range(6)]
    csh1, csc1 = mod[1:2, 0:d], mod[1:2, d:2 * d]

    w_in_b = w_in[0].astype(BF16)
    wcol = w_in_b.shape[1] // 5
    wvt = lax.optimization_barrier(w_in[0, :, 2 * wcol:3 * wcol]).T.astype(BF16)
    n1 = norm1_g[0].reshape(1, d)
    cos, sin = _rope_tables(s)
    q, k, vt, rx, rz = _inproj(x2, n1, sh1, sc1, w_in_b, wvt, cos, sin, _tile(s, 512))
    _, kc, vtc, rxc, _ = _inproj(cx2, n1, csh1, csc1, w_in_b, wvt, jnp.ones((n_ctx, DV), F32),
                                 jnp.zeros((n_ctx, DV), F32), n_ctx)

    attn = _attn(q, kc, k, vtc, vt, diff_lambda[0], subln_g[0].reshape(1, DV), _tile(s, 1024))

    wcat = jnp.concatenate([rg_wa[0], rg_wi[0]], axis=-1).astype(BF16)
    rec_args = (rec_conv_w[0], rec_conv_b[0], wcat, rg_ba[0], rg_bi[0], rg_lambda[0])
    _, _, h_ctx = _rglru(rxc, *rec_args, jnp.zeros((8, rxc.shape[1]), F32), n_ctx)
    hf, hb, _ = _rglru(rx, *rec_args, h_ctx, _tile(s, 256))

    x1, h2 = _outproj(attn, hf, hb, rz, x2, w_out[0].astype(BF16), g1,
                      norm2_g[0].reshape(1, d), sh2, sc2, _tile(s, 512))

    dff = w_up.shape[2]
    tf = 512
    pad = (-dff) % tf
    w_up_b = jnp.pad(w_up[0], ((0, 0), (0, pad))).astype(BF16)
    w_gate_b = jnp.pad(w_gate[0], ((0, 0), (0, pad))).astype(BF16)
    w_down_b = jnp.pad(w_down[0], ((0, pad), (0, 0))).astype(BF16)
    cw = jnp.pad(ffn_conv_w[0], ((0, 0), (0, pad)))
    cb = jnp.pad(ffn_conv_b[0], ((0, pad),)).reshape(1, dff + pad)
    out = _ffn(h2, x1, w_up_b, w_gate_b, w_down_b, cw, cb, g2, final_g.reshape(1, d),
               _tile(s, 512), tf)
    return out[None]
```

```python
import functools
import math

import jax
import jax.numpy as jnp
from jax import lax
from jax.experimental import pallas as pl
from jax.experimental.pallas import tpu as pltpu

F32 = jnp.float32
BF16 = jnp.bfloat16

GRID_W = 64
N_HEADS = 8
DV = 128
HEAD_DIM = DV // 2
N_FREQ = HEAD_DIM // 4
N_REC_BLOCKS = 8
REC_BLOCK = 128
RG_C = 8.0
ROPE_THETA = 10000.0
EPS = 1e-6
SUBLN_EPS = 1e-5
LAM_INIT = 0.8 - 0.6 * math.exp(-0.3 * 0)
LOG2E = 1.4426950408889634
NEG_BIG = -1e30

LANES = 128
SUBLANES = 8
MXU_N = 256
VMEM_LIMIT = 56 * 1024 * 1024

ADA_COLS = 2048
INPROJ_ROWS = 512
ATTN_ROWS = 1024
RGLRU_ROWS = 256
OUTPROJ_ROWS = 512
FFN_ROWS = 512
FFN_COLS = 512


def _params(sem, vmem=VMEM_LIMIT):
    return pltpu.CompilerParams(dimension_semantics=sem, vmem_limit_bytes=vmem)


def _tile(n, pref):
    if n <= pref:
        return n
    t = pref
    while t >= 16:
        if n % t == 0:
            return t
        t -= 16
    return n


def _gelu_tanh(x):
    k0 = -2.0 * math.sqrt(2.0 / math.pi) * LOG2E
    return x / (1.0 + jnp.exp2(x * (k0 + (k0 * 0.044715) * (x * x))))


def _rms(x, eps):
    return x * lax.rsqrt(jnp.mean(x * x, axis=-1, keepdims=True) + eps)


def _ada_kernel(c_ref, w_ref, b_ref, o_ref):
    cc = c_ref[...]
    s = cc * jax.nn.sigmoid(cc)
    o_ref[...] = jnp.dot(s, w_ref[...], preferred_element_type=F32,
                         precision=lax.Precision.HIGHEST) + b_ref[...]


def _ada(cc, w_ada, b_ada):
    d, n = w_ada.shape
    tn = _tile(n, ADA_COLS)
    return pl.pallas_call(
        _ada_kernel,
        grid=(n // tn,),
        in_specs=[pl.BlockSpec((8, d), lambda j: (0, 0)),
                  pl.BlockSpec((d, tn), lambda j: (0, j)),
                  pl.BlockSpec((1, tn), lambda j: (0, j))],
        out_specs=pl.BlockSpec((8, tn), lambda j: (0, j)),
        out_shape=jax.ShapeDtypeStruct((8, n), F32),
        compiler_params=_params(("arbitrary",)),
        name="ada",
    )(cc, w_ada, b_ada.reshape(1, n))


def _inproj_kernel(x_ref, g_ref, sh_ref, sc_ref, w_ref, wvt_ref, cos_ref, sin_ref,
                   q_ref, k_ref, vt_ref, rx_ref, rz_ref, h_scr, *, q_scale):
    j = pl.program_id(1)
    tm = x_ref.shape[0]

    @pl.when(j == 0)
    def _():
        y = _rms(x_ref[...], EPS) * g_ref[...]
        h_scr[...] = (y * (1.0 + sc_ref[...]) + sh_ref[...]).astype(BF16)

    def proj():
        return jnp.dot(h_scr[...], w_ref[...], preferred_element_type=F32)

    def rope_into(o_ref, scale):
        acc = proj()
        cos = cos_ref[...]
        sin = sin_ref[...]
        lane = lax.broadcasted_iota(jnp.int32, (tm, LANES), 1)
        first = (lane % 32) < 16
        for g in range(acc.shape[1] // LANES):
            t = acc[:, g * LANES:(g + 1) * LANES]
            partner = jnp.where(first, pltpu.roll(t, LANES - 16, 1), pltpu.roll(t, 16, 1))
            r = t * cos + partner * sin
            if scale != 1.0:
                r = r * scale
            o_ref[:, g * LANES:(g + 1) * LANES] = r.astype(o_ref.dtype)

    @pl.when(j == 0)
    def _():
        rope_into(q_ref, q_scale)

    @pl.when(j == 1)
    def _():
        rope_into(k_ref, 1.0)

    @pl.when(j == 2)
    def _():
        vt_ref[...] = lax.dot_general(wvt_ref[...], h_scr[...], (((1,), (1,)), ((), ())),
                                      preferred_element_type=F32).astype(vt_ref.dtype)

    @pl.when(j == 3)
    def _():
        rx_ref[...] = proj()

    @pl.when(j == 4)
    def _():
        rz_ref[...] = proj()


def _inproj(x2, g, sh, sc, w_in, wvt, cos, sin, tm):
    s, d = x2.shape
    wcol = w_in.shape[1] // 5
    row = lambda i, j: (i, 0)
    vec = pl.BlockSpec((1, d), lambda i, j: (0, 0))
    out = lambda dt: jax.ShapeDtypeStruct((s, wcol), dt)
    return pl.pallas_call(
        functools.partial(_inproj_kernel, q_scale=HEAD_DIM ** -0.5 * LOG2E),
        grid=(s // tm, 5),
        in_specs=[pl.BlockSpec((tm, d), row), vec, vec, vec,
                  pl.BlockSpec((d, wcol), lambda i, j: (0, jnp.where(j == 2, 1, j))),
                  pl.BlockSpec(wvt.shape, lambda i, j: (0, 0)),
                  pl.BlockSpec((tm, LANES), row), pl.BlockSpec((tm, LANES), row)],
        out_specs=[pl.BlockSpec((tm, wcol), row), pl.BlockSpec((tm, wcol), row),
                   pl.BlockSpec((wcol, tm), lambda i, j: (0, i)),
                   pl.BlockSpec((tm, wcol), row), pl.BlockSpec((tm, wcol), row)],
        out_shape=[out(BF16), out(BF16), jax.ShapeDtypeStruct((wcol, s), BF16), out(F32), out(F32)],
        scratch_shapes=[pltpu.VMEM((tm, d), BF16)],
        compiler_params=_params(("arbitrary", "arbitrary")),
        name="inproj",
    )(x2, g, sh, sc, w_in, wvt, cos, sin)


TKC = MXU_N
QSUB = MXU_N
LOOKAHEAD = 2
OVERFLOW_GUARD = 2.0 ** 100


def _attn_kernel(q_ref, kc_ref, k_ref, vtc_ref, vt_ref, dl_ref, sg_ref, o_ref):
    tq = q_ref.shape[0]
    chunks = ([(kc_ref, vtc_ref, o) for o in range(0, kc_ref.shape[0], TKC)]
              + [(k_ref, vt_ref, o) for o in range(0, k_ref.shape[0], TKC)])
    n_chunks = len(chunks)

    d = dl_ref[...]
    lam = (jnp.exp(jnp.sum(d[0:1] * d[1:2], axis=1, keepdims=True))
           - jnp.exp(jnp.sum(d[2:3] * d[3:4], axis=1, keepdims=True)) + LAM_INIT)

    def q_sub(qs, carry):
        r0 = pl.multiple_of(qs * QSUB, QSUB)
        qt = q_ref[pl.ds(r0, QSUB), :].astype(F32).T
        dim = lax.broadcasted_iota(jnp.int32, qt.shape, 0)
        qqt = jnp.concatenate([jnp.where(dim < HEAD_DIM, qt, 0.0),
                               jnp.where(dim >= HEAD_DIM, qt, 0.0)], axis=1).astype(BF16)

        def scores(c):
            k_src, _, o = chunks[c]
            return jnp.dot(k_src[o:o + TKC, :], qqt, preferred_element_type=F32)

        def for_each_chunk(step):
            pending = {}
            issued = 0
            for c in range(n_chunks):
                while issued < min(c + LOOKAHEAD + 1, n_chunks):
                    pending[issued] = scores(issued)
                    issued += 1
                step(c, pending.pop(c))

        def pv_product(c, st, shift):
            _, v_src, o = chunks[c]
            p = jnp.exp2(st - shift)
            part = jnp.dot(v_src[:, o:o + TKC], p.astype(BF16), preferred_element_type=F32)
            return part, jnp.sum(p.reshape(TKC // SUBLANES, SUBLANES, p.shape[1]), axis=0)

        def fixed_shift():
            state = {}

            def step(c, st):
                if c == 0:
                    state["m"] = jnp.max(st, axis=0, keepdims=True)
                    state["acc"], state["l"] = pv_product(c, st, state["m"])
                else:
                    part, lpart = pv_product(c, st, state["m"])
                    state["acc"] = state["acc"] + part
                    state["l"] = state["l"] + lpart

            for_each_chunk(step)
            return state["acc"], state["l"]

        def running_max():
            state = {"m": jnp.full((1, 2 * QSUB), NEG_BIG, F32),
                     "acc": jnp.zeros((DV, 2 * QSUB), F32),
                     "l": jnp.zeros((SUBLANES, 2 * QSUB), F32)}

            def step(c, st):
                m_new = jnp.maximum(state["m"], jnp.max(st, axis=0, keepdims=True))
                alpha = jnp.exp2(state["m"] - m_new)
                part, lpart = pv_product(c, st, m_new)
                state["acc"] = state["acc"] * alpha + part
                state["l"] = state["l"] * alpha + lpart
                state["m"] = m_new

            for_each_chunk(step)
            return state["acc"], state["l"]

        acc, l8 = fixed_shift()
        finite = lambda t: jnp.max(jnp.where(jnp.abs(t) < OVERFLOW_GUARD, 0.0, 1.0)) < 0.5
        acc, l8 = lax.cond(finite(acc) & finite(l8), lambda: (acc, l8), running_max)

        r = acc / jnp.sum(l8, axis=0, keepdims=True)
        ot = r[:, :QSUB] - lam * r[:, QSUB:]
        ot = ot * lax.rsqrt(jnp.mean(ot * ot, axis=0, keepdims=True) + SUBLN_EPS)
        o_ref[pl.ds(r0, QSUB), :] = (ot.T * sg_ref[...] * (1.0 - LAM_INIT)).astype(o_ref.dtype)
        return carry

    lax.fori_loop(0, tq // QSUB, q_sub, 0)


def _attn(q, kc, k, vtc, vt, diff_lambda, subln_g, tq):
    s = q.shape[0]
    n_ctx = kc.shape[0]
    assert n_ctx % TKC == 0 and s % TKC == 0 and tq % QSUB == 0
    head_rows = lambda n: pl.BlockSpec((n, DV), lambda h, i: (0, h))
    head_cols = lambda n: pl.BlockSpec((DV, n), lambda h, i: (h, 0))
    return pl.pallas_call(
        _attn_kernel,
        grid=(N_HEADS, s // tq),
        in_specs=[pl.BlockSpec((tq, DV), lambda h, i: (i, h)),
                  head_rows(n_ctx), head_rows(s), head_cols(n_ctx), head_cols(s),
                  pl.BlockSpec((4, HEAD_DIM), lambda h, i: (0, 0)),
                  pl.BlockSpec((1, DV), lambda h, i: (0, 0))],
        out_specs=pl.BlockSpec((tq, DV), lambda h, i: (i, h)),
        out_shape=jax.ShapeDtypeStruct((s, N_HEADS * DV), BF16),
        compiler_params=_params(("arbitrary", "arbitrary")),
        name="attn",
    )(q, kc, k, vtc, vt, diff_lambda, subln_g)


def _log1p(u):
    w = 1.0 + u
    return jnp.where(w == 1.0, u, jnp.log(w) * (u / (w - 1.0)))


def _softplus(z):
    return jnp.maximum(z, 0.0) + _log1p(jnp.exp(-jnp.abs(z)))


def _neg_expm1(u, e):
    d = e - 1.0
    safe = jnp.where(d == 0.0, 1.0, jnp.log(e))
    return -jnp.where(d == 0.0, u, d * (u / safe))


def _rglru_kernel(cur_f, prev_f, next_f, cur_b, prev_b, next_b, cw_ref, cb_ref, w_ref,
                  ba_ref, bi_ref, lam_ref, h0_ref, hf_ref, hb_ref, hl_ref,
                  e_scr, a_f, b_f, a_b, b_b, h_scr, *, halo):
    i = pl.program_id(0)
    n = pl.num_programs(0)
    t_rows = cur_f.shape[0]

    @pl.when(i == 0)
    def _():
        h_scr[...] = h0_ref[...]

    def conv(cur, prev, nxt, first, last):
        e_scr[0:halo, :] = jnp.where(first, 0.0, prev[...])
        e_scr[halo:halo + t_rows, :] = cur[...]
        e_scr[halo + t_rows:, :] = jnp.where(last, 0.0, nxt[...])
        y = cb_ref[...] + cw_ref[0:1, :] * e_scr[halo - 2:halo - 2 + t_rows, :]
        y = y + cw_ref[1:2, :] * e_scr[halo - 1:halo - 1 + t_rows, :]
        y = y + cw_ref[2:3, :] * e_scr[halo:halo + t_rows, :]
        y = y + cw_ref[3:4, :] * e_scr[halo + 1:halo + 1 + t_rows, :]
        return y

    def gates(xr, d, a_scr, b_scr):
        xb = xr.astype(BF16)
        sp = _softplus(-lam_ref[d:d + 1, :])
        for k in range(N_REC_BLOCKS):
            sl = slice(k * REC_BLOCK, (k + 1) * REC_BLOCK)
            pre = jnp.dot(xb[:, sl], w_ref[d, k], preferred_element_type=F32)
            r = jax.nn.sigmoid(pre[:, :REC_BLOCK] + ba_ref[d:d + 1, sl])
            ig = jax.nn.sigmoid(pre[:, REC_BLOCK:] + bi_ref[d:d + 1, sl])
            log_a = (-RG_C) * r * sp[:, sl]
            a = jnp.exp(log_a)
            a_scr[:, sl] = a
            b_scr[:, sl] = jnp.sqrt(_neg_expm1(2.0 * log_a, a * a)) * (ig * xr[:, sl])

    gates(conv(cur_f, prev_f, next_f, i == 0, i == n - 1), 0, a_f, b_f)
    gates(conv(cur_b, prev_b, next_b, i == n - 1, i == 0), 1, a_b, b_b)

    def step(t, carry):
        hf, hb = carry
        hf = a_f[pl.ds(t, 1), :] * hf + b_f[pl.ds(t, 1), :]
        hf_ref[pl.ds(t, 1), :] = hf
        tb = t_rows - 1 - t
        hb = a_b[pl.ds(tb, 1), :] * hb + b_b[pl.ds(tb, 1), :]
        hb_ref[pl.ds(tb, 1), :] = hb
        return hf, hb

    hf, hb = lax.fori_loop(0, t_rows, step, (h_scr[0:1, :], h_scr[1:2, :]), unroll=8)
    h_scr[0:1, :] = hf
    h_scr[1:2, :] = hb
    hl_ref[...] = h_scr[...]


def _rglru(rx, cw, cb, wcat, ba, bi, lam, h0, t_rows):
    s, c = rx.shape
    n = s // t_rows
    halo = 8
    bpc = t_rows // halo
    nhb = s // halo
    cur = lambda f: pl.BlockSpec((t_rows, c), lambda i: (f(i, n), 0))
    prv = lambda f: pl.BlockSpec((halo, c), lambda i: (jnp.maximum(f(i, n) * bpc - 1, 0), 0))
    nxt = lambda f: pl.BlockSpec((halo, c), lambda i: (jnp.minimum((f(i, n) + 1) * bpc, nhb - 1), 0))
    fwd = lambda i, n: i
    bwd = lambda i, n: n - 1 - i
    full = lambda shape: pl.BlockSpec(shape, lambda i: (0,) * len(shape))
    return pl.pallas_call(
        functools.partial(_rglru_kernel, halo=halo),
        grid=(n,),
        in_specs=[cur(fwd), prv(fwd), nxt(fwd), cur(bwd), prv(bwd), nxt(bwd),
                  full(cw.shape), full((1, c)), full(wcat.shape),
                  full((2, c)), full((2, c)), full((2, c)), full((8, c))],
        out_specs=[pl.BlockSpec((t_rows, c), lambda i: (i, 0)),
                   pl.BlockSpec((t_rows, c), lambda i: (n - 1 - i, 0)),
                   full((8, c))],
        out_shape=[jax.ShapeDtypeStruct((s, c), F32), jax.ShapeDtypeStruct((s, c), F32),
                   jax.ShapeDtypeStruct((8, c), F32)],
        scratch_shapes=[pltpu.VMEM((t_rows + 2 * halo, c), F32)]
                       + [pltpu.VMEM((t_rows, c), F32)] * 4
                       + [pltpu.VMEM((8, c), F32)],
        compiler_params=_params(("arbitrary",)),
        name="rglru",
    )(rx, rx, rx, rx, rx, rx, cw, cb.reshape(1, c), wcat, ba, bi, lam, h0)


def _outproj_kernel(attn_ref, hf_ref, hb_ref, rz_ref, x_ref, w_ref, g1_ref, n2_ref,
                    sh_ref, sc_ref, x1_ref, h2_ref):
    half = attn_ref.shape[1]
    rec = ((hf_ref[...] + hb_ref[...]) * _gelu_tanh(rz_ref[...])).astype(BF16)
    y = jnp.dot(attn_ref[...], w_ref[:half, :], preferred_element_type=F32)
    y = y + jnp.dot(rec, w_ref[half:, :], preferred_element_type=F32)
    x1 = x_ref[...] + g1_ref[...] * y
    x1_ref[...] = x1
    h2 = _rms(x1, EPS) * n2_ref[...]
    h2_ref[...] = (h2 * (1.0 + sc_ref[...]) + sh_ref[...]).astype(BF16)


def _outproj(attn, hf, hb, rz, x2, w_out, g1, n2, sh2, sc2, tm):
    s, d = x2.shape
    half = attn.shape[1]
    row = lambda i: (i, 0)
    vec = pl.BlockSpec((1, d), lambda i: (0, 0))
    return pl.pallas_call(
        _outproj_kernel,
        grid=(s // tm,),
        in_specs=[pl.BlockSpec((tm, half), row)] * 4
                 + [pl.BlockSpec((tm, d), row), pl.BlockSpec(w_out.shape, lambda i: (0, 0)),
                    vec, vec, vec, vec],
        out_specs=[pl.BlockSpec((tm, d), row), pl.BlockSpec((tm, d), row)],
        out_shape=[jax.ShapeDtypeStruct((s, d), F32), jax.ShapeDtypeStruct((s, d), BF16)],
        compiler_params=_params(("arbitrary",)),
        name="outproj",
    )(attn, hf, hb, rz, x2, w_out, g1, n2, sh2, sc2)


def _ffn_kernel(h_ref, hp_ref, hn_ref, x1_ref, wu_ref, wg_ref, wd_ref, cw_ref, cb_ref,
                g2_ref, fg_ref, o_ref, he, acc, *, halo):
    i = pl.program_id(0)
    f = pl.program_id(1)
    tm = h_ref.shape[0]

    @pl.when(f == 0)
    def _():
        zero = jnp.zeros(hp_ref.shape, BF16)
        he[0:halo, :] = jnp.where(i == 0, zero, hp_ref[...])
        he[halo:halo + tm, :] = h_ref[...]
        he[halo + tm:, :] = jnp.where(i == pl.num_programs(0) - 1, zero, hn_ref[...])
        acc[...] = jnp.zeros(acc.shape, F32)

    u = jnp.dot(h_ref[...], wu_ref[...], preferred_element_type=F32)
    g = jnp.dot(he[...], wg_ref[...], preferred_element_type=F32)
    rows = g.shape[0]
    gm = pltpu.roll(g, 1, 0)
    gp = pltpu.roll(g, rows - 1, 0)
    conv = cb_ref[...] + cw_ref[0:1, :] * gm + cw_ref[1:2, :] * g + cw_ref[2:3, :] * gp
    act = (_gelu_tanh(conv[halo:halo + tm, :]) * u).astype(BF16)
    acc[...] += jnp.dot(act, wd_ref[...], preferred_element_type=F32)

    @pl.when(f == pl.num_programs(1) - 1)
    def _():
        x2 = x1_ref[...] + g2_ref[...] * acc[...]
        o_ref[...] = _rms(x2, EPS) * fg_ref[...]


def _ffn(h2, x1, w_up, w_gate, w_down, cw, cb, g2, fg, tm, tf):
    s, d = h2.shape
    dff = w_up.shape[1]
    halo = 16
    bpc = tm // halo
    nhb = s // halo
    row = lambda i, f: (i, 0)
    vec = pl.BlockSpec((1, d), lambda i, f: (0, 0))
    return pl.pallas_call(
        functools.partial(_ffn_kernel, halo=halo),
        grid=(s // tm, dff // tf),
        in_specs=[pl.BlockSpec((tm, d), row),
                  pl.BlockSpec((halo, d), lambda i, f: (jnp.maximum(i * bpc - 1, 0), 0)),
                  pl.BlockSpec((halo, d), lambda i, f: (jnp.minimum((i + 1) * bpc, nhb - 1), 0)),
                  pl.BlockSpec((tm, d), row),
                  pl.BlockSpec((d, tf), lambda i, f: (0, f)),
                  pl.BlockSpec((d, tf), lambda i, f: (0, f)),
                  pl.BlockSpec((tf, d), lambda i, f: (f, 0)),
                  pl.BlockSpec((3, tf), lambda i, f: (0, f)),
                  pl.BlockSpec((1, tf), lambda i, f: (0, f)),
                  vec, vec],
        out_specs=pl.BlockSpec((tm, d), row),
        out_shape=jax.ShapeDtypeStruct((s, d), F32),
        scratch_shapes=[pltpu.VMEM((tm + 2 * halo, d), BF16), pltpu.VMEM((tm, d), F32)],
        compiler_params=_params(("arbitrary", "arbitrary")),
        name="ffn",
    )(h2, h2, h2, x1, w_up, w_gate, w_down, cw, cb, g2, fg)


def _rope_tables(s):
    n_rows = s // GRID_W
    lane = jnp.arange(DV)
    inv = ROPE_THETA ** (-(lane % N_FREQ).astype(F32) / N_FREQ)
    is_col = (lane % HEAD_DIM) >= HEAD_DIM // 2
    sign = jnp.where((lane % (2 * N_FREQ)) < N_FREQ, -1.0, 1.0)
    ang_r = jnp.arange(n_rows, dtype=F32)[:, None] * inv
    ang_c = jnp.arange(GRID_W, dtype=F32)[:, None] * inv
    pick = lambda by_row, by_col: jnp.where(is_col, by_col[None, :, :], by_row[:, None, :]).reshape(s, DV)
    return (pick(jnp.cos(ang_r), jnp.cos(ang_c)),
            pick(jnp.sin(ang_r) * sign, jnp.sin(ang_c) * sign))


def kernel(x, c, ctx, c_ctx, w_ada, b_ada, norm1_g, w_in, rec_conv_w, rec_conv_b, rg_wa, rg_ba,
           rg_wi, rg_bi, rg_lambda, diff_lambda, subln_g, w_out, norm2_g, w_up, w_gate,
           ffn_conv_w, ffn_conv_b, w_down, final_g):
    assert x.shape[0] == 1 and w_ada.shape[0] == 1, "single batch element, single layer"
    s, d = x.shape[1:]
    n_ctx = ctx.shape[1]
    x2 = x[0]
    cx2 = ctx[0]

    cc = jnp.zeros((8, d), F32).at[0].set(c[0]).at[1].set(c_ctx)
    mod = _ada(cc, w_ada[0], b_ada[0])
    sh1, sc1, g1, sh2, sc2, g2 = [mod[0:1, j * d:(j + 1) * d] for j in range(6)]
    csh1, csc1 = mod[1:2, 0:d], mod[1:2, d:2 * d]

    w_in_b = w_in[0].astype(BF16)
    wcol = w_in_b.shape[1] // 5
    wvt = lax.optimization_barrier(w_in[0, :, 2 * wcol:3 * wcol]).T.astype(BF16)
    n1 = norm1_g[0].reshape(1, d)
    cos, sin = _rope_tables(s)
    q, k, vt, rx, rz = _inproj(x2, n1, sh1, sc1, w_in_b, wvt, cos, sin, _tile(s, INPROJ_ROWS))
    _, kc, vtc, rxc, _ = _inproj(cx2, n1, csh1, csc1, w_in_b, wvt, jnp.ones((n_ctx, DV), F32),
                                 jnp.zeros((n_ctx, DV), F32), n_ctx)

    attn = _attn(q, kc, k, vtc, vt, diff_lambda[0], subln_g[0].reshape(1, DV), _tile(s, ATTN_ROWS))

    wcat = jnp.concatenate([rg_wa[0], rg_wi[0]], axis=-1).astype(BF16)
    rec_args = (rec_conv_w[0], rec_conv_b[0], wcat, rg_ba[0], rg_bi[0], rg_lambda[0])
    _, _, h_ctx = _rglru(rxc, *rec_args, jnp.zeros((8, rxc.shape[1]), F32), n_ctx)
    hf, hb, _ = _rglru(rx, *rec_args, h_ctx, _tile(s, RGLRU_ROWS))

    x1, h2 = _outproj(attn, hf, hb, rz, x2, w_out[0].astype(BF16), g1,
                      norm2_g[0].reshape(1, d), sh2, sc2, _tile(s, OUTPROJ_ROWS))

    dff = w_up.shape[2]
    tf = FFN_COLS
    pad = (-dff) % tf
    w_up_b = jnp.pad(w_up[0], ((0, 0), (0, pad))).astype(BF16)
    w_gate_b = jnp.pad(w_gate[0], ((0, 0), (0, pad))).astype(BF16)
    w_down_b = jnp.pad(w_down[0], ((0, pad), (0, 0))).astype(BF16)
    cw = jnp.pad(ffn_conv_w[0], ((0, 0), (0, pad)))
    cb = jnp.pad(ffn_conv_b[0], ((0, pad),)).reshape(1, dff + pad)
    out = _ffn(h2, x1, w_up_b, w_gate_b, w_down_b, cw, cb, g2, final_g.reshape(1, d),
               _tile(s, FFN_ROWS), tf)
    return out[None]
```

```python
import functools
import math

import jax
import jax.numpy as jnp
from jax import lax
from jax.experimental import pallas as pl
from jax.experimental.pallas import tpu as pltpu

F32 = jnp.float32
BF16 = jnp.bfloat16

GRID_W = 64
N_HEADS = 8
DV = 128
HEAD_DIM = DV // 2
N_FREQ = HEAD_DIM // 4
N_REC_BLOCKS = 8
REC_BLOCK = 128
RG_C = 8.0
ROPE_THETA = 10000.0
EPS = 1e-6
SUBLN_EPS = 1e-5
LAM_INIT = 0.8 - 0.6 * math.exp(-0.3 * 0)
LOG2E = 1.4426950408889634
NEG_BIG = -1e30

LANES = 128
SUBLANES = 8
MXU_N = 256
VMEM_LIMIT = 56 * 1024 * 1024

ADA_COLS = 2048
INPROJ_ROWS = 512
ATTN_ROWS = 1024
RGLRU_ROWS = 256
OUTPROJ_ROWS = 512
FFN_ROWS = 512
FFN_COLS = 512


def _params(sem, vmem=VMEM_LIMIT):
    return pltpu.CompilerParams(dimension_semantics=sem, vmem_limit_bytes=vmem)


def _tile(n, pref):
    if n <= pref:
        return n
    t = pref
    while t >= 16:
        if n % t == 0:
            return t
        t -= 16
    return n


def _gelu_tanh(x):
    k0 = -2.0 * math.sqrt(2.0 / math.pi) * LOG2E
    return x / (1.0 + jnp.exp2(x * (k0 + (k0 * 0.044715) * (x * x))))


def _rms(x, eps):
    return x * lax.rsqrt(jnp.mean(x * x, axis=-1, keepdims=True) + eps)


def _ada_kernel(c_ref, w_ref, b_ref, o_ref):
    cc = c_ref[...]
    s = cc * jax.nn.sigmoid(cc)
    o_ref[...] = jnp.dot(s, w_ref[...], preferred_element_type=F32,
                         precision=lax.Precision.HIGHEST) + b_ref[...]


def _ada(cc, w_ada, b_ada):
    d, n = w_ada.shape
    tn = _tile(n, ADA_COLS)
    return pl.pallas_call(
        _ada_kernel,
        grid=(n // tn,),
        in_specs=[pl.BlockSpec((8, d), lambda j: (0, 0)),
                  pl.BlockSpec((d, tn), lambda j: (0, j)),
                  pl.BlockSpec((1, tn), lambda j: (0, j))],
        out_specs=pl.BlockSpec((8, tn), lambda j: (0, j)),
        out_shape=jax.ShapeDtypeStruct((8, n), F32),
        compiler_params=_params(("arbitrary",)),
        name="ada",
    )(cc, w_ada, b_ada.reshape(1, n))


def _inproj_kernel(x_ref, g_ref, sh_ref, sc_ref, w_ref, wvt_ref, cos_ref, sin_ref,
                   q_ref, k_ref, vt_ref, rx_ref, rz_ref, h_scr, *, q_scale):
    j = pl.program_id(1)
    tm = x_ref.shape[0]

    @pl.when(j == 0)
    def _():
        y = _rms(x_ref[...], EPS) * g_ref[...]
        h_scr[...] = (y * (1.0 + sc_ref[...]) + sh_ref[...]).astype(BF16)

    def proj():
        return jnp.dot(h_scr[...], w_ref[...], preferred_element_type=F32)

    def rope_into(o_ref, scale):
        acc = proj()
        cos = cos_ref[...]
        sin = sin_ref[...]
        lane = lax.broadcasted_iota(jnp.int32, (tm, LANES), 1)
        first = (lane % 32) < 16
        for g in range(acc.shape[1] // LANES):
            t = acc[:, g * LANES:(g + 1) * LANES]
            partner = jnp.where(first, pltpu.roll(t, LANES - 16, 1), pltpu.roll(t, 16, 1))
            r = t * cos + partner * sin
            if scale != 1.0:
                r = r * scale
            o_ref[:, g * LANES:(g + 1) * LANES] = r.astype(o_ref.dtype)

    @pl.when(j == 0)
    def _():
        rope_into(q_ref, q_scale)

    @pl.when(j == 1)
    def _():
        rope_into(k_ref, 1.0)

    @pl.when(j == 2)
    def _():
        vt_ref[...] = lax.dot_general(wvt_ref[...], h_scr[...], (((1,), (1,)), ((), ())),
                                      preferred_element_type=F32).astype(vt_ref.dtype)

    @pl.when(j == 3)
    def _():
        rx_ref[...] = proj()

    @pl.when(j == 4)
    def _():
        rz_ref[...] = proj()


def _inproj(x2, g, sh, sc, w_in, wvt, cos, sin, tm):
    s, d = x2.shape
    wcol = w_in.shape[1] // 5
    row = lambda i, j: (i, 0)
    vec = pl.BlockSpec((1, d), lambda i, j: (0, 0))
    out = lambda dt: jax.ShapeDtypeStruct((s, wcol), dt)
    return pl.pallas_call(
        functools.partial(_inproj_kernel, q_scale=HEAD_DIM ** -0.5 * LOG2E),
        grid=(s // tm, 5),
        in_specs=[pl.BlockSpec((tm, d), row), vec, vec, vec,
                  pl.BlockSpec((d, wcol), lambda i, j: (0, jnp.where(j == 2, 1, j))),
                  pl.BlockSpec(wvt.shape, lambda i, j: (0, 0)),
                  pl.BlockSpec((tm, LANES), row), pl.BlockSpec((tm, LANES), row)],
        out_specs=[pl.BlockSpec((tm, wcol), row), pl.BlockSpec((tm, wcol), row),
                   pl.BlockSpec((wcol, tm), lambda i, j: (0, i)),
                   pl.BlockSpec((tm, wcol), row), pl.BlockSpec((tm, wcol), row)],
        out_shape=[out(BF16), out(BF16), jax.ShapeDtypeStruct((wcol, s), BF16), out(F32), out(F32)],
        scratch_shapes=[pltpu.VMEM((tm, d), BF16)],
        compiler_params=_params(("arbitrary", "arbitrary")),
        name="inproj",
    )(x2, g, sh, sc, w_in, wvt, cos, sin)


TKC = MXU_N
QSUB = MXU_N
LOOKAHEAD = 2
OVERFLOW_GUARD = 2.0 ** 100


def _attn_kernel(q_ref, kc_ref, k_ref, vtc_ref, vt_ref, dl_ref, sg_ref, o_ref):
    tq = q_ref.shape[0]
    chunks = ([(kc_ref, vtc_ref, o) for o in range(0, kc_ref.shape[0], TKC)]
              + [(k_ref, vt_ref, o) for o in range(0, k_ref.shape[0], TKC)])
    n_chunks = len(chunks)

    d = dl_ref[...]
    lam = (jnp.exp(jnp.sum(d[0:1] * d[1:2], axis=1, keepdims=True))
           - jnp.exp(jnp.sum(d[2:3] * d[3:4], axis=1, keepdims=True)) + LAM_INIT)

    def q_sub(qs, carry):
        r0 = pl.multiple_of(qs * QSUB, QSUB)
        qt = q_ref[pl.ds(r0, QSUB), :].astype(F32).T
        dim = lax.broadcasted_iota(jnp.int32, qt.shape, 0)
        qqt = jnp.concatenate([jnp.where(dim < HEAD_DIM, qt, 0.0),
                               jnp.where(dim >= HEAD_DIM, qt, 0.0)], axis=1).astype(BF16)

        def scores(c):
            k_src, _, o = chunks[c]
            return jnp.dot(k_src[o:o + TKC, :], qqt, preferred_element_type=F32)

        def for_each_chunk(step):
            pending = {}
            issued = 0
            for c in range(n_chunks):
                while issued < min(c + LOOKAHEAD + 1, n_chunks):
                    pending[issued] = scores(issued)
                    issued += 1
                step(c, pending.pop(c))

        def pv_product(c, st, shift):
            _, v_src, o = chunks[c]
            p = jnp.exp2(st - shift)
            part = jnp.dot(v_src[:, o:o + TKC], p.astype(BF16), preferred_element_type=F32)
            return part, jnp.sum(p.reshape(TKC // SUBLANES, SUBLANES, p.shape[1]), axis=0)

        def fixed_shift():
            state = {}

            def step(c, st):
                if c == 0:
                    state["m"] = jnp.max(st, axis=0, keepdims=True)
                    state["acc"], state["l"] = pv_product(c, st, state["m"])
                else:
                    part, lpart = pv_product(c, st, state["m"])
                    state["acc"] = state["acc"] + part
                    state["l"] = state["l"] + lpart

            for_each_chunk(step)
            return state["acc"], state["l"]

        def running_max():
            state = {"m": jnp.full((1, 2 * QSUB), NEG_BIG, F32),
                     "acc": jnp.zeros((DV, 2 * QSUB), F32),
                     "l": jnp.zeros((SUBLANES, 2 * QSUB), F32)}

            def step(c, st):
                m_new = jnp.maximum(state["m"], jnp.max(st, axis=0, keepdims=True))
                alpha = jnp.exp2(state["m"] - m_new)
                part, lpart = pv_product(c, st, m_new)
                state["acc"] = state["acc"] * alpha + part
                state["l"] = state["l"] * alpha + lpart
                state["m"] = m_new

            for_each_chunk(step)
            return state["acc"], state["l"]

        acc, l8 = fixed_shift()
        finite = lambda t: jnp.max(jnp.where(jnp.abs(t) < OVERFLOW_GUARD, 0.0, 1.0)) < 0.5
        acc, l8 = lax.cond(finite(acc) & finite(l8), lambda: (acc, l8), running_max)

        r = acc / jnp.sum(l8, axis=0, keepdims=True)
        ot = r[:, :QSUB] - lam * r[:, QSUB:]
        ot = ot * lax.rsqrt(jnp.mean(ot * ot, axis=0, keepdims=True) + SUBLN_EPS)
        o_ref[pl.ds(r0, QSUB), :] = (ot.T * sg_ref[...] * (1.0 - LAM_INIT)).astype(o_ref.dtype)
        return carry

    lax.fori_loop(0, tq // QSUB, q_sub, 0)


def _attn(q, kc, k, vtc, vt, diff_lambda, subln_g, tq):
    s = q.shape[0]
    n_ctx = kc.shape[0]
    assert n_ctx % TKC == 0 and s % TKC == 0 and tq % QSUB == 0
    head_rows = lambda n: pl.BlockSpec((n, DV), lambda h, i: (0, h))
    head_cols = lambda n: pl.BlockSpec((DV, n), lambda h, i: (h, 0))
    return pl.pallas_call(
        _attn_kernel,
        grid=(N_HEADS, s // tq),
        in_specs=[pl.BlockSpec((tq, DV), lambda h, i: (i, h)),
                  head_rows(n_ctx), head_rows(s), head_cols(n_ctx), head_cols(s),
                  pl.BlockSpec((4, HEAD_DIM), lambda h, i: (0, 0)),
                  pl.BlockSpec((1, DV), lambda h, i: (0, 0))],
        out_specs=pl.BlockSpec((tq, DV), lambda h, i: (i, h)),
        out_shape=jax.ShapeDtypeStruct((s, N_HEADS * DV), BF16),
        compiler_params=_params(("arbitrary", "arbitrary")),
        name="attn",
    )(q, kc, k, vtc, vt, diff_lambda, subln_g)


def _log1p(u):
    w = 1.0 + u
    return jnp.where(w == 1.0, u, jnp.log(w) * (u / (w - 1.0)))


def _softplus(z):
    return jnp.maximum(z, 0.0) + _log1p(jnp.exp(-jnp.abs(z)))


def _neg_expm1(u, e):
    d = e - 1.0
    safe = jnp.where(d == 0.0, 1.0, jnp.log(e))
    return -jnp.where(d == 0.0, u, d * (u / safe))


def _rglru_kernel(cur_f, prev_f, next_f, cur_b, prev_b, next_b, cw_ref, cb_ref, w_ref,
                  ba_ref, bi_ref, lam_ref, h0_ref, hf_ref, hb_ref, hl_ref,
                  e_scr, a_f, b_f, a_b, b_b, h_scr, *, halo):
    i = pl.program_id(0)
    n = pl.num_programs(0)
    t_rows = cur_f.shape[0]

    @pl.when(i == 0)
    def _():
        h_scr[...] = h0_ref[...]

    def conv(cur, prev, nxt, first, last):
        e_scr[0:halo, :] = jnp.where(first, 0.0, prev[...])
        e_scr[halo:halo + t_rows, :] = cur[...]
        e_scr[halo + t_rows:, :] = jnp.where(last, 0.0, nxt[...])
        y = cb_ref[...] + cw_ref[0:1, :] * e_scr[halo - 2:halo - 2 + t_rows, :]
        y = y + cw_ref[1:2, :] * e_scr[halo - 1:halo - 1 + t_rows, :]
        y = y + cw_ref[2:3, :] * e_scr[halo:halo + t_rows, :]
        y = y + cw_ref[3:4, :] * e_scr[halo + 1:halo + 1 + t_rows, :]
        return y

    def gates(xr, d, a_scr, b_scr):
        xb = xr.astype(BF16)
        sp = _softplus(-lam_ref[d:d + 1, :])
        for k in range(N_REC_BLOCKS):
            sl = slice(k * REC_BLOCK, (k + 1) * REC_BLOCK)
            pre = jnp.dot(xb[:, sl], w_ref[d, k], preferred_element_type=F32)
            r = jax.nn.sigmoid(pre[:, :REC_BLOCK] + ba_ref[d:d + 1, sl])
            ig = jax.nn.sigmoid(pre[:, REC_BLOCK:] + bi_ref[d:d + 1, sl])
            log_a = (-RG_C) * r * sp[:, sl]
            a = jnp.exp(log_a)
            a_scr[:, sl] = a
            b_scr[:, sl] = jnp.sqrt(_neg_expm1(2.0 * log_a, a * a)) * (ig * xr[:, sl])

    gates(conv(cur_f, prev_f, next_f, i == 0, i == n - 1), 0, a_f, b_f)
    gates(conv(cur_b, prev_b, next_b, i == n - 1, i == 0), 1, a_b, b_b)

    def step(t, carry):
        hf, hb = carry
        hf = a_f[pl.ds(t, 1), :] * hf + b_f[pl.ds(t, 1), :]
        hf_ref[pl.ds(t, 1), :] = hf
        tb = t_rows - 1 - t
        hb = a_b[pl.ds(tb, 1), :] * hb + b_b[pl.ds(tb, 1), :]
        hb_ref[pl.ds(tb, 1), :] = hb
        return hf, hb

    hf, hb = lax.fori_loop(0, t_rows, step, (h_scr[0:1, :], h_scr[1:2, :]), unroll=8)
    h_scr[0:1, :] = hf
    h_scr[1:2, :] = hb
    hl_ref[...] = h_scr[...]


def _rglru(rx, cw, cb, wcat, ba, bi, lam, h0, t_rows):
    s, c = rx.shape
    n = s // t_rows
    halo = 8
    bpc = t_rows // halo
    nhb = s // halo
    cur = lambda f: pl.BlockSpec((t_rows, c), lambda i: (f(i, n), 0))
    prv = lambda f: pl.BlockSpec((halo, c), lambda i: (jnp.maximum(f(i, n) * bpc - 1, 0), 0))
    nxt = lambda f: pl.BlockSpec((halo, c), lambda i: (jnp.minimum((f(i, n) + 1) * bpc, nhb - 1), 0))
    fwd = lambda i, n: i
    bwd = lambda i, n: n - 1 - i
    full = lambda shape: pl.BlockSpec(shape, lambda i: (0,) * len(shape))
    return pl.pallas_call(
        functools.partial(_rglru_kernel, halo=halo),
        grid=(n,),
        in_specs=[cur(fwd), prv(fwd), nxt(fwd), cur(bwd), prv(bwd), nxt(bwd),
                  full(cw.shape), full((1, c)), full(wcat.shape),
                  full((2, c)), full((2, c)), full((2, c)), full((8, c))],
        out_specs=[pl.BlockSpec((t_rows, c), lambda i: (i, 0)),
                   pl.BlockSpec((t_rows, c), lambda i: (n - 1 - i, 0)),
                   full((8, c))],
        out_shape=[jax.ShapeDtypeStruct((s, c), F32), jax.ShapeDtypeStruct((s, c), F32),
                   jax.ShapeDtypeStruct((8, c), F32)],
        scratch_shapes=[pltpu.VMEM((t_rows + 2 * halo, c), F32)]
                       + [pltpu.VMEM((t_rows, c), F32)] * 4
                       + [pltpu.VMEM((8, c), F32)],
        compiler_params=_params(("arbitrary",)),
        name="rglru",
    )(rx, rx, rx, rx, rx, rx, cw, cb.reshape(1, c), wcat, ba, bi, lam, h0)


def _outproj_kernel(attn_ref, hf_ref, hb_ref, rz_ref, x_ref, w_ref, g1_ref, n2_ref,
                    sh_ref, sc_ref, x1_ref, h2_ref):
    half = attn_ref.shape[1]
    rec = ((hf_ref[...] + hb_ref[...]) * _gelu_tanh(rz_ref[...])).astype(BF16)
    y = jnp.dot(attn_ref[...], w_ref[:half, :], preferred_element_type=F32)
    y = y + jnp.dot(rec, w_ref[half:, :], preferred_element_type=F32)
    x1 = x_ref[...] + g1_ref[...] * y
    x1_ref[...] = x1
    h2 = _rms(x1, EPS) * n2_ref[...]
    h2_ref[...] = (h2 * (1.0 + sc_ref[...]) + sh_ref[...]).astype(BF16)


def _outproj(attn, hf, hb, rz, x2, w_out, g1, n2, sh2, sc2, tm):
    s, d = x2.shape
    half = attn.shape[1]
    row = lambda i: (i, 0)
    vec = pl.BlockSpec((1, d), lambda i: (0, 0))
    return pl.pallas_call(
        _outproj_kernel,
        grid=(s // tm,),
        in_specs=[pl.BlockSpec((tm, half), row)] * 4
                 + [pl.BlockSpec((tm, d), row), pl.BlockSpec(w_out.shape, lambda i: (0, 0)),
                    vec, vec, vec, vec],
        out_specs=[pl.BlockSpec((tm, d), row), pl.BlockSpec((tm, d), row)],
        out_shape=[jax.ShapeDtypeStruct((s, d), F32), jax.ShapeDtypeStruct((s, d), BF16)],
        compiler_params=_params(("arbitrary",)),
        name="outproj",
    )(attn, hf, hb, rz, x2, w_out, g1, n2, sh2, sc2)


def _ffn_kernel(h_ref, hp_ref, hn_ref, x1_ref, wu_ref, wg_ref, wd_ref, cw_ref, cb_ref,
                g2_ref, fg_ref, o_ref, he, acc, *, halo):
    i = pl.program_id(0)
    f = pl.program_id(1)
    tm = h_ref.shape[0]

    @pl.when(f == 0)
    def _():
        zero = jnp.zeros(hp_ref.shape, BF16)
        he[0:halo, :] = jnp.where(i == 0, zero, hp_ref[...])
        he[halo:halo + tm, :] = h_ref[...]
        he[halo + tm:, :] = jnp.where(i == pl.num_programs(0) - 1, zero, hn_ref[...])
        acc[...] = jnp.zeros(acc.shape, F32)

    u = jnp.dot(h_ref[...], wu_ref[...], preferred_element_type=F32)
    g = jnp.dot(he[...], wg_ref[...], preferred_element_type=F32)
    rows = g.shape[0]
    gm = pltpu.roll(g, 1, 0)
    gp = pltpu.roll(g, rows - 1, 0)
    down = None
    for a in range(u.shape[1] // MXU_N):
        sl = slice(a * MXU_N, (a + 1) * MXU_N)
        conv = (cb_ref[:, sl] + cw_ref[0:1, sl] * gm[:, sl] + cw_ref[1:2, sl] * g[:, sl]
                + cw_ref[2:3, sl] * gp[:, sl])
        act = (_gelu_tanh(conv[halo:halo + tm, :]) * u[:, sl]).astype(BF16)
        part = jnp.dot(act, wd_ref[sl, :], preferred_element_type=F32)
        down = part if down is None else down + part
    acc[...] += down

    @pl.when(f == pl.num_programs(1) - 1)
    def _():
        x2 = x1_ref[...] + g2_ref[...] * acc[...]
        o_ref[...] = _rms(x2, EPS) * fg_ref[...]


def _ffn(h2, x1, w_up, w_gate, w_down, cw, cb, g2, fg, tm, tf):
    s, d = h2.shape
    dff = w_up.shape[1]
    halo = 16
    bpc = tm // halo
    nhb = s // halo
    row = lambda i, f: (i, 0)
    vec = pl.BlockSpec((1, d), lambda i, f: (0, 0))
    return pl.pallas_call(
        functools.partial(_ffn_kernel, halo=halo),
        grid=(s // tm, dff // tf),
        in_specs=[pl.BlockSpec((tm, d), row),
                  pl.BlockSpec((halo, d), lambda i, f: (jnp.maximum(i * bpc - 1, 0), 0)),
                  pl.BlockSpec((halo, d), lambda i, f: (jnp.minimum((i + 1) * bpc, nhb - 1), 0)),
                  pl.BlockSpec((tm, d), row),
                  pl.BlockSpec((d, tf), lambda i, f: (0, f)),
                  pl.BlockSpec((d, tf), lambda i, f: (0, f)),
                  pl.BlockSpec((tf, d), lambda i, f: (f, 0)),
                  pl.BlockSpec((3, tf), lambda i, f: (0, f)),
                  pl.BlockSpec((1, tf), lambda i, f: (0, f)),
                  vec, vec],
        out_specs=pl.BlockSpec((tm, d), row),
        out_shape=jax.ShapeDtypeStruct((s, d), F32),
        scratch_shapes=[pltpu.VMEM((tm + 2 * halo, d), BF16), pltpu.VMEM((tm, d), F32)],
        compiler_params=_params(("arbitrary", "arbitrary")),
        name="ffn",
    )(h2, h2, h2, x1, w_up, w_gate, w_down, cw, cb, g2, fg)


def _rope_tables(s):
    n_rows = s // GRID_W
    lane = jnp.arange(DV)
    inv = ROPE_THETA ** (-(lane % N_FREQ).astype(F32) / N_FREQ)
    is_col = (lane % HEAD_DIM) >= HEAD_DIM // 2
    sign = jnp.where((lane % (2 * N_FREQ)) < N_FREQ, -1.0, 1.0)
    ang_r = jnp.arange(n_rows, dtype=F32)[:, None] * inv
    ang_c = jnp.arange(GRID_W, dtype=F32)[:, None] * inv
    pick = lambda by_row, by_col: jnp.where(is_col, by_col[None, :, :], by_row[:, None, :]).reshape(s, DV)
    return (pick(jnp.cos(ang_r), jnp.cos(ang_c)),
            pick(jnp.sin(ang_r) * sign, jnp.sin(ang_c) * sign))


def kernel(x, c, ctx, c_ctx, w_ada, b_ada, norm1_g, w_in, rec_conv_w, rec_conv_b, rg_wa, rg_ba,
           rg_wi, rg_bi, rg_lambda, diff_lambda, subln_g, w_out, norm2_g, w_up, w_gate,
           ffn_conv_w, ffn_conv_b, w_down, final_g):
    assert x.shape[0] == 1 and w_ada.shape[0] == 1, "single batch element, single layer"
    s, d = x.shape[1:]
    n_ctx = ctx.shape[1]
    x2 = x[0]
    cx2 = ctx[0]

    cc = jnp.zeros((8, d), F32).at[0].set(c[0]).at[1].set(c_ctx)
    mod = _ada(cc, w_ada[0], b_ada[0])
    sh1, sc1, g1, sh2, sc2, g2 = [mod[0:1, j * d:(j + 1) * d] for j in range(6)]
    csh1, csc1 = mod[1:2, 0:d], mod[1:2, d:2 * d]

    w_in_b = w_in[0].astype(BF16)
    wcol = w_in_b.shape[1] // 5
    wvt = lax.optimization_barrier(w_in[0, :, 2 * wcol:3 * wcol]).T.astype(BF16)
    n1 = norm1_g[0].reshape(1, d)
    cos, sin = _rope_tables(s)
    q, k, vt, rx, rz = _inproj(x2, n1, sh1, sc1, w_in_b, wvt, cos, sin, _tile(s, INPROJ_ROWS))
    _, kc, vtc, rxc, _ = _inproj(cx2, n1, csh1, csc1, w_in_b, wvt, jnp.ones((n_ctx, DV), F32),
                                 jnp.zeros((n_ctx, DV), F32), n_ctx)

    attn = _attn(q, kc, k, vtc, vt, diff_lambda[0], subln_g[0].reshape(1, DV), _tile(s, ATTN_ROWS))

    wcat = jnp.concatenate([rg_wa[0], rg_wi[0]], axis=-1).astype(BF16)
    rec_args = (rec_conv_w[0], rec_conv_b[0], wcat, rg_ba[0], rg_bi[0], rg_lambda[0])
    _, _, h_ctx = _rglru(rxc, *rec_args, jnp.zeros((8, rxc.shape[1]), F32), n_ctx)
    hf, hb, _ = _rglru(rx, *rec_args, h_ctx, _tile(s, RGLRU_ROWS))

    x1, h2 = _outproj(attn, hf, hb, rz, x2, w_out[0].astype(BF16), g1,
                      norm2_g[0].reshape(1, d), sh2, sc2, _tile(s, OUTPROJ_ROWS))

    dff = w_up.shape[2]
    tf = FFN_COLS
    pad = (-dff) % tf
    w_up_b = jnp.pad(w_up[0], ((0, 0), (0, pad))).astype(BF16)
    w_gate_b = jnp.pad(w_gate[0], ((0, 0), (0, pad))).astype(BF16)
    w_down_b = jnp.pad(w_down[0], ((0, pad), (0, 0))).astype(BF16)
    cw = jnp.pad(ffn_conv_w[0], ((0, 0), (0, pad)))
    cb = jnp.pad(ffn_conv_b[0], ((0, pad),)).reshape(1, dff + pad)
    out = _ffn(h2, x1, w_up_b, w_gate_b, w_down_b, cw, cb, g2, final_g.reshape(1, d),
               _tile(s, FFN_ROWS), tf)
    return out[None]
```
